```python
import jax
import jax.numpy as jnp
from jax import lax
import numpy as np

D_MODEL = 4096
BATCH = 4
SEQ = 4096
DEPTH = 2
DEC_BATCH = 4
DEC_SEQ = 2048
PAST_LEN = 128

N_BRANCH = 4
BRANCH_WIDTH = D_MODEL // 4
HEAD_DIM = 128
N_HEADS = BRANCH_WIDTH // HEAD_DIM
GMLP_CHUNK = 128
HGRN_CHUNK = 64
CONV_WIDTH = 3
MLA_Q_RANK = (3 * D_MODEL) // 16
MLA_KV_RANK = D_MODEL // 16
MLA_NOPE = 128
MLA_ROPE = 64
MLA_V = 128
MLA_QK = MLA_NOPE + MLA_ROPE
ROPE_BASE = 10000.0
ATTN_BLOCK = 128
N_GROUPS = 8
EXPERTS_PER_GROUP = 8
N_EXPERTS = N_GROUPS * EXPERTS_PER_GROUP
TOP_K = 2
EXPERT_FF = D_MODEL // 4
MOE_BLOCK = 128
EPS = 1e-6

IN_SIZES = (2 * BRANCH_WIDTH, 5 * BRANCH_WIDTH, 3 * BRANCH_WIDTH, MLA_Q_RANK, MLA_KV_RANK, MLA_ROPE, N_BRANCH * D_MODEL)
N_IN = sum(IN_SIZES)
IN_SPLITS = tuple(int(s) for s in np.cumsum(IN_SIZES)[:-1])

kernel_name = 'hybrid_bidir_encoder_hmoe'


def rms_norm(x, g):
    xf = x.astype(jnp.float32)
    y = xf * lax.rsqrt(jnp.mean(xf * xf, axis=-1, keepdims=True) + EPS)
    return (y * g.astype(jnp.float32)).astype(x.dtype)


def gmlp_mixer(p, v_gain, w_s, b_s):
    Bn, S, _ = p.shape
    u, v = jnp.split(jax.nn.gelu(p), 2, axis=-1)
    v = rms_norm(v, v_gain)
    v = v.reshape(Bn, S // GMLP_CHUNK, GMLP_CHUNK, N_HEADS, HEAD_DIM)
    mixed = jnp.einsum('bnphc,hqp->bnqhc', v, w_s) + jnp.swapaxes(b_s, 0, 1)[:, :, None]
    return u * mixed.reshape(Bn, S, BRANCH_WIDTH)


def chunk_gated_scan(q, k, v, log_f):
    Bn, H, S, dk = q.shape
    dv = v.shape[-1]
    n = S // HGRN_CHUNK

    def to_chunks(t):
        return jnp.moveaxis(t.reshape(Bn, H, n, HGRN_CHUNK, t.shape[-1]), 2, 0)

    causal = jnp.tril(jnp.ones((HGRN_CHUNK, HGRN_CHUNK), dtype=bool))

    def step(state, inp):
        qc, kc, vc, fc = inp
        a = jnp.cumsum(fc, axis=-2)
        o_inter = jnp.einsum('bhld,bhde->bhle', qc * jnp.exp(a), state)
        diff = a[:, :, :, None, :] - a[:, :, None, :, :]
        decay = jnp.exp(jnp.where(causal[:, :, None], diff, -jnp.inf))
        scores = jnp.einsum('bhtsd,bhsd->bhts', qc[:, :, :, None, :] * decay, kc)
        o_intra = jnp.einsum('bhts,bhse->bhte', scores, vc)
        a_last = a[:, :, -1:, :]
        new_state = jnp.exp(a_last[:, :, 0, :])[..., None] * state + jnp.einsum('bhld,bhle->bhde', kc * jnp.exp(a_last - a), vc)
        return new_state, o_inter + o_intra

    s0 = jnp.zeros((Bn, H, dk, dv), jnp.float32)
    _, out = lax.scan(step, s0, (to_chunks(q), to_chunks(k), to_chunks(v), to_chunks(log_f)))
    return jnp.moveaxis(out, 0, 2).reshape(Bn, H, S, dv)


def hgrn2_mixer(p, lb_f, lb_b, out_gain):
    Bn, S, _ = p.shape
    f32 = jnp.float32
    q, i, zf, zb, g = jnp.split(p, 5, axis=-1)

    def heads(t):
        return t.astype(f32).reshape(Bn, S, N_HEADS, HEAD_DIM).transpose(0, 2, 1, 3)

    def log_forget(z, lb):
        return jnp.logaddexp(jnp.log(lb), jnp.log1p(-lb) + jax.nn.log_sigmoid(z.astype(f32)))

    def flip(t):
        return jnp.flip(t, axis=2)

    qh, ih = heads(q), heads(i)
    lf_f = heads(log_forget(zf, lb_f))
    lf_b = flip(heads(log_forget(zb, lb_b)))
    o_fwd = chunk_gated_scan(qh, -jnp.expm1(lf_f), ih, lf_f)
    o_bwd = flip(chunk_gated_scan(flip(qh), -jnp.expm1(lf_b), flip(ih), lf_b))
    o = (o_fwd + o_bwd).transpose(0, 2, 1, 3)
    gh = g.astype(f32).reshape(Bn, S, N_HEADS, HEAD_DIM)
    o = rms_norm(o, out_gain) * jax.nn.silu(gh)
    return o.reshape(Bn, S, BRANCH_WIDTH).astype(p.dtype)


def short_conv_mixer(p, conv_w):
    bg, cg, h = jnp.split(p, 3, axis=-1)
    z = cg * h
    y = lax.conv_general_dilated(z, conv_w[:, None, :], window_strides=(1,),
                                 padding=((CONV_WIDTH // 2, CONV_WIDTH // 2),),
                                 dimension_numbers=('NWC', 'WIO', 'NWC'),
                                 feature_group_count=BRANCH_WIDTH)
    return bg * y


def apply_rope(x, positions):
    half = MLA_ROPE // 2
    inv_freq = ROPE_BASE ** (-jnp.arange(half, dtype=jnp.float32) * 2.0 / MLA_ROPE)
    ang = positions[:, None] * inv_freq[None, :]
    cos = jnp.cos(ang)[:, None, :]
    sin = jnp.sin(ang)[:, None, :]
    xf = x.astype(jnp.float32)
    x1, x2 = xf[..., :half], xf[..., half:]
    return jnp.concatenate([x1 * cos - x2 * sin, x1 * sin + x2 * cos], axis=-1).astype(x.dtype)


def mla_mixer(c_q, c_kv, k_rope, q_gain, kv_gain, w_uq, w_ukv, gq, gk):
    Bn, S, _ = c_q.shape
    q = (rms_norm(c_q, q_gain) @ w_uq).reshape(Bn, S, N_HEADS, MLA_QK)
    kv = (rms_norm(c_kv, kv_gain) @ w_ukv).reshape(Bn, S, N_HEADS, MLA_NOPE + MLA_V)
    k_nope, v = kv[..., :MLA_NOPE], kv[..., MLA_NOPE:]
    k = jnp.concatenate([k_nope, jnp.broadcast_to(k_rope[:, :, None, :], (Bn, S, N_HEADS, MLA_ROPE))], axis=-1)
    q = rms_norm(q, gq)
    k = rms_norm(k, gk)
    pos = jnp.arange(S, dtype=jnp.float32)
    q = jnp.concatenate([q[..., :MLA_NOPE], apply_rope(q[..., MLA_NOPE:], pos)], axis=-1)
    k = jnp.concatenate([k[..., :MLA_NOPE], apply_rope(k[..., MLA_NOPE:], pos)], axis=-1)
    nb = S // ATTN_BLOCK
    qb = q.reshape(Bn, nb, ATTN_BLOCK, N_HEADS, MLA_QK).transpose(1, 0, 2, 3, 4)
    scale = MLA_QK ** -0.5

    def attend(q_blk):
        s = jnp.einsum('bqhd,bkhd->bhqk', q_blk, k).astype(jnp.float32) * scale
        pr = jax.nn.softmax(s, axis=-1).astype(v.dtype)
        return jnp.einsum('bhqk,bkhd->bqhd', pr, v)

    o = lax.map(attend, qb)
    return o.transpose(1, 0, 2, 3, 4).reshape(Bn, S, N_HEADS * MLA_V)


def hier_moe(x, w_group, b_group, w_router, b_router, w_gate, w_up, w_down):
    Bn, S, D = x.shape
    t = x.reshape(-1, D)
    N = t.shape[0]
    g_prob = jax.nn.softmax((t @ w_group).astype(jnp.float32) + b_group.astype(jnp.float32), axis=-1)
    g_top, g_idx = lax.top_k(g_prob, 1)
    e_logits = ((t @ w_router).astype(jnp.float32) + b_router.astype(jnp.float32)).reshape(N, N_GROUPS, EXPERTS_PER_GROUP)
    e_in_group = e_logits[jnp.arange(N), g_idx[:, 0]]
    e_val, e_idx = lax.top_k(e_in_group, TOP_K)
    gate = g_top * jax.nn.softmax(e_val, axis=-1)
    eid = g_idx * EXPERTS_PER_GROUP + e_idx

    M = N * TOP_K
    flat_e = eid.reshape(-1)
    flat_tok = jnp.repeat(jnp.arange(N, dtype=jnp.int32), TOP_K)
    flat_w = gate.reshape(-1)
    order = jnp.argsort(flat_e)
    sorted_e = flat_e[order]
    counts = jnp.bincount(flat_e, length=N_EXPERTS)
    padded = (counts + MOE_BLOCK - 1) // MOE_BLOCK * MOE_BLOCK
    pad_end = jnp.cumsum(padded)
    pad_start = pad_end - padded
    start = jnp.cumsum(counts) - counts
    dest = pad_start[sorted_e] + jnp.arange(M) - start[sorted_e]
    P = M + N_EXPERTS * MOE_BLOCK
    tok_buf = jnp.zeros((P,), jnp.int32).at[dest].set(flat_tok[order])
    w_buf = jnp.zeros((P,), x.dtype).at[dest].set(flat_w[order].astype(x.dtype))
    n_blocks = P // MOE_BLOCK
    block_e = jnp.minimum(jnp.searchsorted(pad_end, jnp.arange(n_blocks) * MOE_BLOCK, side='right'), N_EXPERTS - 1)
    xs = t[tok_buf].reshape(n_blocks, MOE_BLOCK, D)

    def expert_block(args):
        xb, e = args
        h = jax.nn.silu(xb @ w_gate[e]) * (xb @ w_up[e])
        return h @ w_down[e]

    ys = lax.map(expert_block, (xs, block_e)).reshape(P, D)
    out = jnp.zeros_like(t).at[tok_buf].add(ys * w_buf[:, None])
    return out.reshape(Bn, S, D)


def encoder_layer(x, lb_f, lb_b, norm_mix, w_in, gmlp_v_norm, gmlp_w_s, gmlp_b_s, hgrn_out_norm, conv_w,
                  mla_q_norm, mla_kv_norm, mla_w_uq, mla_w_ukv, qk_norm_q, qk_norm_k, w_branch, w_out,
                  norm_ffn, w_group, b_group, w_router, b_router, w_gate, w_up, w_down):
    Bn, S, D = x.shape
    h = rms_norm(x, norm_mix)
    proj = h @ w_in
    pa, pb, pc, cq, ckv, kr, gl = jnp.split(proj, IN_SPLITS, axis=-1)
    branches = (
        gmlp_mixer(pa, gmlp_v_norm, gmlp_w_s, gmlp_b_s),
        hgrn2_mixer(pb, lb_f, lb_b, hgrn_out_norm),
        short_conv_mixer(pc, conv_w),
        mla_mixer(cq, ckv, kr, mla_q_norm, mla_kv_norm, mla_w_uq, mla_w_ukv, qk_norm_q, qk_norm_k),
    )
    gates = jax.nn.sigmoid(gl.astype(jnp.float32)).astype(x.dtype).reshape(Bn, S, N_BRANCH, D)
    merged = jnp.zeros_like(x)
    for kb in range(N_BRANCH):
        merged = merged + gates[:, :, kb] * (branches[kb] @ w_branch[kb])
    x = x + merged @ w_out
    x = x + hier_moe(rms_norm(x, norm_ffn), w_group, b_group, w_router, b_router, w_gate, w_up, w_down)
    return x


def setup_inputs(seed: int = 0) -> dict:
    key = jax.random.key(seed)
    ks = jax.random.split(key, 26)
    f32 = jnp.float32
    L = DEPTH

    def nrm(k, shape, scale):
        return jax.random.normal(k, shape, f32) * scale

    def gain(k, shape):
        return 1.0 + 0.05 * jax.random.normal(k, shape, f32)

    return {
        'x_prompt': nrm(ks[0], (BATCH, SEQ, D_MODEL), 1.0),
        'x_sample': nrm(ks[1], (DEC_BATCH, DEC_SEQ, D_MODEL), 1.0),
        'norm_mix': gain(ks[2], (L, D_MODEL)),
        'w_in': nrm(ks[3], (L, D_MODEL, N_IN), D_MODEL ** -0.5),
        'gmlp_v_norm': gain(ks[4], (L, BRANCH_WIDTH)),
        'gmlp_w_s': nrm(ks[5], (L, N_HEADS, GMLP_CHUNK, GMLP_CHUNK), GMLP_CHUNK ** -0.5),
        'gmlp_b_s': 1.0 + nrm(ks[6], (L, N_HEADS, GMLP_CHUNK), 0.02),
        'hgrn_lb_logits': nrm(ks[7], (2, L, BRANCH_WIDTH), 0.5),
        'hgrn_out_norm': gain(ks[8], (L, HEAD_DIM)),
        'conv_w': nrm(ks[9], (L, CONV_WIDTH, BRANCH_WIDTH), CONV_WIDTH ** -0.5),
        'mla_q_norm': gain(ks[10], (L, MLA_Q_RANK)),
        'mla_kv_norm': gain(ks[11], (L, MLA_KV_RANK)),
        'mla_w_uq': nrm(ks[12], (L, MLA_Q_RANK, N_HEADS * MLA_QK), MLA_Q_RANK ** -0.5),
        'mla_w_ukv': nrm(ks[13], (L, MLA_KV_RANK, N_HEADS * (MLA_NOPE + MLA_V)), MLA_KV_RANK ** -0.5),
        'qk_norm_q': gain(ks[14], (L, MLA_QK)),
        'qk_norm_k': gain(ks[15], (L, MLA_QK)),
        'w_branch': nrm(ks[16], (L, N_BRANCH, BRANCH_WIDTH, D_MODEL), BRANCH_WIDTH ** -0.5),
        'w_out': nrm(ks[17], (L, D_MODEL, D_MODEL), D_MODEL ** -0.5),
        'norm_ffn': gain(ks[18], (L, D_MODEL)),
        'w_group': nrm(ks[19], (L, D_MODEL, N_GROUPS), D_MODEL ** -0.5),
        'b_group': nrm(ks[20], (L, N_GROUPS), 0.01),
        'w_router': nrm(ks[21], (L, D_MODEL, N_EXPERTS), D_MODEL ** -0.5),
        'b_router': nrm(ks[22], (L, N_EXPERTS), 0.01),
        'w_gate': nrm(ks[23], (L, N_EXPERTS, D_MODEL, EXPERT_FF), D_MODEL ** -0.5),
        'w_up': nrm(ks[24], (L, N_EXPERTS, D_MODEL, EXPERT_FF), D_MODEL ** -0.5),
        'w_down': nrm(ks[25], (L, N_EXPERTS, EXPERT_FF, D_MODEL), EXPERT_FF ** -0.5),
    }


def reference(x_prompt, x_sample, norm_mix, w_in, gmlp_v_norm, gmlp_w_s, gmlp_b_s, hgrn_lb_logits,
              hgrn_out_norm, conv_w, mla_q_norm, mla_kv_norm, mla_w_uq, mla_w_ukv, qk_norm_q, qk_norm_k,
              w_branch, w_out, norm_ffn, w_group, b_group, w_router, b_router, w_gate, w_up, w_down):
    lb_p = jax.nn.softmax(hgrn_lb_logits.astype(jnp.float32), axis=1)
    lb = jnp.maximum(jnp.cumsum(lb_p, axis=1) - lb_p[:, :1], 0.0)

    def trunk(x):
        for l in range(DEPTH):
            x = encoder_layer(x, lb[0, l], lb[1, l], norm_mix[l], w_in[l], gmlp_v_norm[l], gmlp_w_s[l],
                              gmlp_b_s[l], hgrn_out_norm[l], conv_w[l], mla_q_norm[l], mla_kv_norm[l],
                              mla_w_uq[l], mla_w_ukv[l], qk_norm_q[l], qk_norm_k[l], w_branch[l], w_out[l],
                              norm_ffn[l], w_group[l], b_group[l], w_router[l], b_router[l],
                              w_gate[l], w_up[l], w_down[l])
        return x

    y_prompt = trunk(x_prompt)
    y_sample = trunk(x_sample)
    return (y_prompt, y_sample)
```

```python
import functools
import math

import jax
import jax.numpy as jnp
import numpy as np
from jax import lax
from jax.experimental import pallas as pl
from jax.experimental.pallas import tpu as pltpu

F32 = jnp.float32
BF16 = jnp.bfloat16

D_MODEL = 4096
BRANCH_WIDTH = 1024
HEAD_DIM = 128
N_HEADS = 8
GMLP_CHUNK = 128
MLA_Q_RANK = 768
MLA_KV_RANK = 256
MLA_NOPE = 128
MLA_ROPE = 64
MLA_QK = MLA_NOPE + MLA_ROPE
MLA_QK_PAD = 256
ROPE_BASE = 10000.0
N_GROUPS = 8
EXPERTS_PER_GROUP = 8
N_EXPERTS = 64
TOP_K = 2
EXPERT_FF = 1024
EPS = 1e-6
ABC_WIDTH = 10 * BRANCH_WIDTH
QKV_WIDTH = MLA_Q_RANK + MLA_KV_RANK + MLA_ROPE
QKV_PAD = 1152
LANES = 128
ROW_WORDS = D_MODEL // LANES
VMEM_LIMIT = 56 * 1024 * 1024
HGRN_CHUNK = 64
MOE_BLOCK = 256
NEG_INF = float("-inf")


def _cparams(sem):
    return pltpu.CompilerParams(dimension_semantics=sem, vmem_limit_bytes=VMEM_LIMIT)


def _tile(n, t):
    t = min(t, n)
    while n % t:
        t //= 2
    return t


def _dot(a, b):
    return jnp.dot(a, b, preferred_element_type=F32)


def _dot_nt(a, b):
    return lax.dot_general(a, b, (((1,), (1,)), ((), ())), preferred_element_type=F32)


def _rmsnorm_kernel(x_ref, g_ref, o_ref):
    x = x_ref[...]
    ms = jnp.mean(x * x, axis=-1, keepdims=True)
    o_ref[...] = (x * lax.rsqrt(ms + EPS) * g_ref[...]).astype(o_ref.dtype)


def rmsnorm_rows(x, g, tm=512):
    n, d = x.shape
    tm = _tile(n, tm)
    return pl.pallas_call(
        _rmsnorm_kernel,
        grid=(n // tm,),
        in_specs=[pl.BlockSpec((tm, d), lambda i: (i, 0)),
                  pl.BlockSpec((1, d), lambda i: (0, 0))],
        out_specs=pl.BlockSpec((tm, d), lambda i: (i, 0)),
        out_shape=jax.ShapeDtypeStruct((n, d), BF16),
        compiler_params=_cparams(("parallel",)),
        name="rmsnorm",
    )(x, g.reshape(1, d))


def _mm_kernel(x_ref, w_ref, *rest, act, has_res):
    o_ref = rest[-1]
    acc = _dot(x_ref[...], w_ref[...])
    if act == "sigmoid":
        acc = jax.nn.sigmoid(acc)
    if has_res:
        acc = acc + rest[0][...]
    o_ref[...] = acc.astype(o_ref.dtype)


def matmul(x, w, *, out_dtype, tm, tn, act=None, res=None, name="matmul"):
    m, k = x.shape
    n = w.shape[1]
    tm = _tile(m, tm)
    in_specs = [pl.BlockSpec((tm, k), lambda i, j: (i, 0)),
                pl.BlockSpec((k, tn), lambda i, j: (0, j))]
    args = [x, w]
    if res is not None:
        in_specs.append(pl.BlockSpec((tm, tn), lambda i, j: (i, j)))
        args.append(res)
    return pl.pallas_call(
        functools.partial(_mm_kernel, act=act, has_res=res is not None),
        grid=(m // tm, n // tn),
        in_specs=in_specs,
        out_specs=pl.BlockSpec((tm, tn), lambda i, j: (i, j)),
        out_shape=jax.ShapeDtypeStruct((m, n), out_dtype),
        compiler_params=_cparams(("parallel", "parallel")),
        name=name,
    )(*args)


def _merge_kernel(b0, b1, b2, b3, wb_ref, g0, g1, g2, g3, o_ref):
    acc = None
    for kb, (b_ref, g_ref) in enumerate(((b0, g0), (b1, g1), (b2, g2), (b3, g3))):
        t = _dot(b_ref[...], wb_ref[kb]) * g_ref[...].astype(F32)
        acc = t if acc is None else acc + t
    o_ref[...] = acc.astype(o_ref.dtype)


def merge_branches(branches, w_branch, gates, tm=1024, tn=512):
    n = branches[0].shape[0]
    tm = _tile(n, tm)
    nj = D_MODEL // tn
    b_spec = pl.BlockSpec((tm, BRANCH_WIDTH), lambda i, j: (i, 0))
    g_specs = [pl.BlockSpec((tm, tn), functools.partial(lambda i, j, kb: (i, kb * nj + j), kb=kb))
               for kb in range(4)]
    return pl.pallas_call(
        _merge_kernel,
        grid=(n // tm, nj),
        in_specs=[b_spec] * 4 + [pl.BlockSpec((4, BRANCH_WIDTH, tn), lambda i, j: (0, 0, j))] + g_specs,
        out_specs=pl.BlockSpec((tm, tn), lambda i, j: (i, j)),
        out_shape=jax.ShapeDtypeStruct((n, D_MODEL), BF16),
        compiler_params=_cparams(("parallel", "parallel")),
        name="merge_branches",
    )(*branches, w_branch, gates, gates, gates, gates)


def _gmlp_kernel(u_ref, v_ref, g_ref, ws_ref, bs_ref, o_ref, *, tm):
    for c in range(tm // GMLP_CHUNK):
        rows = pl.ds(c * GMLP_CHUNK, GMLP_CHUNK)
        u = jax.nn.gelu(u_ref[rows, :])
        v = jax.nn.gelu(v_ref[rows, :])
        ms = jnp.mean(v * v, axis=-1, keepdims=True)
        vb = (v * lax.rsqrt(ms + EPS) * g_ref[...]).astype(BF16)
        for h in range(N_HEADS):
            cols = slice(h * HEAD_DIM, (h + 1) * HEAD_DIM)
            mixed = _dot(ws_ref[h], vb[:, cols]) + bs_ref[:, cols]
            o_ref[rows, cols] = (u[:, cols] * mixed).astype(o_ref.dtype)


def gmlp_mixer(pabc, v_gain, w_s, b_s, tm=512):
    n = pabc.shape[0]
    tm = _tile(n, tm)
    bias = jnp.repeat(jnp.swapaxes(b_s, 0, 1), HEAD_DIM, axis=1)
    return pl.pallas_call(
        functools.partial(_gmlp_kernel, tm=tm),
        grid=(n // tm,),
        in_specs=[pl.BlockSpec((tm, BRANCH_WIDTH), lambda i: (i, 0)),
                  pl.BlockSpec((tm, BRANCH_WIDTH), lambda i: (i, 1)),
                  pl.BlockSpec((1, BRANCH_WIDTH), lambda i: (0, 0)),
                  pl.BlockSpec((N_HEADS, GMLP_CHUNK, GMLP_CHUNK), lambda i: (0, 0, 0)),
                  pl.BlockSpec((GMLP_CHUNK, BRANCH_WIDTH), lambda i: (0, 0))],
        out_specs=pl.BlockSpec((tm, BRANCH_WIDTH), lambda i: (i, 0)),
        out_shape=jax.ShapeDtypeStruct((n, BRANCH_WIDTH), BF16),
        compiler_params=_cparams(("parallel",)),
        name="gmlp_mixer",
    )(pabc, pabc, v_gain.reshape(1, -1), w_s.astype(BF16), bias)


def _segment_of(row0, segs):
    seq_len = jnp.int32(segs[-1][2])
    first = jnp.int32(segs[-1][0])
    for (f, nr, s) in reversed(segs[:-1]):
        inside = row0 < f + nr
        seq_len = jnp.where(inside, s, seq_len)
        first = jnp.where(inside, f, first)
    return seq_len, first


def _conv_kernel(bg_ref, cg_ref, h_ref, cgp_ref, hp_ref, cgn_ref, hn_ref, w_ref, o_ref, *, tm, segs):
    row0 = pl.program_id(0) * tm
    seq_len, first = _segment_of(row0, segs)
    at_start = lax.rem(row0 - first, seq_len) == 0
    at_end = lax.rem(row0 - first + tm, seq_len) == 0
    z = cg_ref[...] * h_ref[...]
    left = jnp.where(at_start, 0.0, cgp_ref[7:8, :] * hp_ref[7:8, :])
    right = jnp.where(at_end, 0.0, cgn_ref[0:1, :] * hn_ref[0:1, :])
    ridx = lax.broadcasted_iota(jnp.int32, z.shape, 0)
    z_prev = jnp.where(ridx == 0, left, pltpu.roll(z, 1, axis=0))
    z_next = jnp.where(ridx == tm - 1, right, pltpu.roll(z, tm - 1, axis=0))
    y = w_ref[0:1, :] * z_prev + w_ref[1:2, :] * z + w_ref[2:3, :] * z_next
    o_ref[...] = (bg_ref[...] * y).astype(o_ref.dtype)


def conv_mixer(pabc, conv_w, segs, tm=256):
    n = pabc.shape[0]
    tm = _tile(n, tm)
    nb8 = n // 8
    blk = lambda c: pl.BlockSpec((tm, BRANCH_WIDTH), lambda i: (i, c))
    prev8 = lambda c: pl.BlockSpec((8, BRANCH_WIDTH), lambda i: (jnp.maximum(i * (tm // 8) - 1, 0), c))
    next8 = lambda c: pl.BlockSpec((8, BRANCH_WIDTH), lambda i: (jnp.minimum((i + 1) * (tm // 8), nb8 - 1), c))
    return pl.pallas_call(
        functools.partial(_conv_kernel, tm=tm, segs=segs),
        grid=(n // tm,),
        in_specs=[blk(7), blk(8), blk(9), prev8(8), prev8(9), next8(8), next8(9),
                  pl.BlockSpec((3, BRANCH_WIDTH), lambda i: (0, 0))],
        out_specs=pl.BlockSpec((tm, BRANCH_WIDTH), lambda i: (i, 0)),
        out_shape=jax.ShapeDtypeStruct((n, BRANCH_WIDTH), BF16),
        compiler_params=_cparams(("parallel",)),
        name="conv_mixer",
    )(pabc, pabc, pabc, pabc, pabc, pabc, pabc, conv_w)


def _hgrn_constants(c, reverse):
    n_lev = int(math.log2(c))
    t = np.arange(c)[:, None]
    u = np.arange(c)[None, :]
    if not reverse:
        w_a = u <= t
        w_kd = u > t
    else:
        w_a = u >= t
        w_kd = u < t
    ws = [w_a, w_kd]
    masks = [np.eye(c, dtype=bool)]
    qsel = []
    for lev in range(n_lev):
        m = c >> (lev + 1)
        blk_t = t // (2 * m)
        upper_t = (t % (2 * m)) >= m
        p_last_lower = blk_t * 2 * m + m - 1
        if not reverse:
            w_q = (u > p_last_lower) & (u <= t)
            w_k = (u > t) & (u <= p_last_lower)
            w = np.where(upper_t, w_q, w_k)
            query_t = upper_t
        else:
            p_first_upper = p_last_lower + 1
            w_q = (u >= t) & (u < p_first_upper)
            w_k = (u >= p_first_upper) & (u < t)
            w = np.where(upper_t, w_k, w_q)
            query_t = ~upper_t
        ws.append(w)
        same_blk = blk_t == blk_t.T
        masks.append(same_blk & query_t & (~query_t).T)
        qsel.append(np.broadcast_to(query_t, (c, HEAD_DIM)))
    w_all = np.concatenate(ws, axis=0).astype(np.float32)
    return (jnp.asarray(w_all, BF16), jnp.asarray(np.stack(masks), F32),
            jnp.asarray(np.stack(qsel), F32))


def _log_forget_and_key(z, log_lb, log1m_lb):
    log_sig = jnp.minimum(z, 0.0) - jnp.log1p(jnp.exp(-jnp.abs(z)))
    x2 = log1m_lb + log_sig
    hi = jnp.maximum(log_lb, x2)
    lf = hi + jnp.log1p(jnp.exp(-jnp.abs(log_lb - x2)))
    return lf, jnp.exp(x2 - z)


def _split3(x):
    p1 = x.astype(BF16)
    r1 = x - p1.astype(F32)
    p2 = r1.astype(BF16)
    p3 = (r1 - p2.astype(F32)).astype(BF16)
    return p1, p2, p3


def _hgrn_kernel(q_ref, v_ref, z_ref, *rest, c, n_lev, reverse, segs, final):
    if final:
        g_ref, of_ref, gain_ref = rest[:3]
        rest = rest[3:]
    lb_ref, wall_ref, mask_ref, qsel_ref, o_ref, st_ref = rest
    step = pl.program_id(0)
    n_steps = pl.num_programs(0)
    chunk = (n_steps - 1 - step) if reverse else step
    row0 = chunk * c
    seq_len, first = _segment_of(row0, segs)
    if reverse:
        fresh = lax.rem(row0 - first + c, seq_len) == 0
    else:
        fresh = lax.rem(row0 - first, seq_len) == 0

    @pl.when(fresh)
    def _():
        st_ref[...] = jnp.zeros_like(st_ref)

    w_all = wall_ref[...]
    for h in range(N_HEADS):
        cols = slice(h * HEAD_DIM, (h + 1) * HEAD_DIM)
        q = q_ref[:, cols]
        v = v_ref[:, cols]
        lf, k = _log_forget_and_key(z_ref[:, cols], lb_ref[0:1, cols], lb_ref[1:2, cols])
        p1, p2, p3 = _split3(lf)
        e_all = _dot(w_all, p1) + _dot(w_all, p2) + _dot(w_all, p3)
        a = e_all[0:c]
        qd = (q * jnp.exp(a)).astype(BF16)
        kd = (k * jnp.exp(e_all[c:2 * c])).astype(BF16)
        qb = q.astype(BF16)
        kb = k.astype(BF16)
        scores = mask_ref[0] * _dot_nt(qb, kb)
        for lev in range(n_lev):
            dm = jnp.exp(e_all[(2 + lev) * c:(3 + lev) * c])
            xm = (jnp.where(qsel_ref[lev] > 0.5, q, k) * dm).astype(BF16)
            scores = scores + mask_ref[lev + 1] * _dot_nt(xm, xm)
        st = st_ref[h]
        o = _dot_nt(qd, st.astype(BF16)) + _dot(scores.astype(BF16), v.astype(BF16))
        a_end = a[0:1] if reverse else a[c - 1:c]
        st_ref[h] = st * jnp.exp(a_end) + _dot(v.T.astype(BF16), kd)
        if final:
            o = o + of_ref[:, cols]
            ms = jnp.mean(o * o, axis=-1, keepdims=True)
            o = o * lax.rsqrt(ms + EPS) * gain_ref[...] * jax.nn.silu(g_ref[:, cols])
        o_ref[:, cols] = o.astype(o_ref.dtype)


def _hgrn_pass(pabc, z_col, log_lb2, segs, reverse, extra, c=HGRN_CHUNK):
    n = pabc.shape[0]
    n_lev = int(math.log2(c))
    n_chunks = n // c
    w_all, masks, qsel = _hgrn_constants(c, reverse)
    if reverse:
        row = lambda col: pl.BlockSpec((c, BRANCH_WIDTH), lambda i: (n_chunks - 1 - i, col))
    else:
        row = lambda col: pl.BlockSpec((c, BRANCH_WIDTH), lambda i: (i, col))
    const2 = lambda shape: pl.BlockSpec(shape, lambda i: (0, 0))
    const3 = lambda shape: pl.BlockSpec(shape, lambda i: (0, 0, 0))
    in_specs = [row(2), row(3), row(z_col)]
    args = [pabc, pabc, pabc]
    final = extra is not None
    if final:
        o_fwd, out_gain = extra
        in_specs += [row(6), row(0), const2((1, HEAD_DIM))]
        args += [pabc, o_fwd, out_gain.reshape(1, HEAD_DIM)]
    in_specs += [const2((2, BRANCH_WIDTH)), const2(w_all.shape), const3(masks.shape), const3(qsel.shape)]
    args += [log_lb2, w_all, masks, qsel]
    return pl.pallas_call(
        functools.partial(_hgrn_kernel, c=c, n_lev=n_lev, reverse=reverse, segs=segs, final=final),
        grid=(n_chunks,),
        in_specs=in_specs,
        out_specs=row(0),
        out_shape=jax.ShapeDtypeStruct((n, BRANCH_WIDTH), BF16 if final else F32),
        scratch_shapes=[pltpu.VMEM((N_HEADS, HEAD_DIM, HEAD_DIM), F32)],
        compiler_params=_cparams(("arbitrary",)),
        name="hgrn_bwd" if reverse else "hgrn_fwd",
    )(*args)


def hgrn_mixer(pabc, lb_f, lb_b, out_gain, segs):
    lbs_f = jnp.stack([jnp.log(lb_f), jnp.log1p(-lb_f)])
    lbs_b = jnp.stack([jnp.log(lb_b), jnp.log1p(-lb_b)])
    o_fwd = _hgrn_pass(pabc, 4, lbs_f, segs, False, None)
    return _hgrn_pass(pabc, 5, lbs_b, segs, True, (o_fwd, out_gain))


def _rope_tile(r, cos_t, sin_a, sin_b):
    return r * cos_t + pltpu.roll(r, 96, axis=1) * sin_a + pltpu.roll(r, 32, axis=1) * sin_b


def _mla_prep_kernel(cq_ref, ckv_ref, wq_ref, wkv_ref, qg_ref, kvg_ref, gq_ref, gk_ref,
                     cos_ref, sina_ref, sinb_ref, q_out, k_out, v_out):
    cq = cq_ref[...]
    hq = (cq * lax.rsqrt(jnp.mean(cq * cq, axis=-1, keepdims=True) + EPS) * qg_ref[...]).astype(BF16)
    qf = _dot(hq, wq_ref[...])
    ckvkr = ckv_ref[...]
    ckv = ckvkr[:, :MLA_KV_RANK]
    kr = ckvkr[:, MLA_KV_RANK:]
    hk = (ckv * lax.rsqrt(jnp.mean(ckv * ckv, axis=-1, keepdims=True) + EPS) * kvg_ref[...]).astype(BF16)
    kvf = _dot(hk, wkv_ref[...])
    cos_t, sin_a, sin_b = cos_ref[...], sina_ref[...], sinb_ref[...]
    gq_n, gq_r = gq_ref[:, :MLA_NOPE], gq_ref[:, MLA_NOPE:]
    gk_n, gk_r = gk_ref[:, :MLA_NOPE], gk_ref[:, MLA_NOPE:]
    kr_ss = jnp.sum(kr * kr, axis=-1, keepdims=True)
    kr_rot = _rope_tile(kr * gk_r, cos_t, sin_a, sin_b)
    scale = MLA_QK ** -0.5
    for h in range(N_HEADS):
        base = h * MLA_QK_PAD
        qn = qf[:, base:base + MLA_NOPE]
        qr = qf[:, base + MLA_NOPE:base + MLA_QK_PAD]
        ss = jnp.sum(qn * qn, axis=-1, keepdims=True) + jnp.sum(qr * qr, axis=-1, keepdims=True)
        rstd = lax.rsqrt(ss * (1.0 / MLA_QK) + EPS) * scale
        q_out[h, :, :MLA_NOPE] = (qn * rstd * gq_n).astype(BF16)
        q_out[h, :, MLA_NOPE:] = _rope_tile(qr * rstd * gq_r, cos_t, sin_a, sin_b).astype(BF16)
        kn = kvf[:, base:base + MLA_NOPE]
        ssk = jnp.sum(kn * kn, axis=-1, keepdims=True) + kr_ss
        rstdk = lax.rsqrt(ssk * (1.0 / MLA_QK) + EPS)
        k_out[h, :, :MLA_NOPE] = (kn * rstdk * gk_n).astype(BF16)
        k_out[h, :, MLA_NOPE:] = (kr_rot * rstdk).astype(BF16)
        v_out[h] = kvf[:, base + MLA_NOPE:base + MLA_QK_PAD].astype(BF16)


def _rope_tables(segs):
    half = MLA_ROPE // 2
    inv_freq = ROPE_BASE ** (-jnp.arange(half, dtype=F32) * 2.0 / MLA_ROPE)
    pos = jnp.concatenate([jnp.tile(jnp.arange(s, dtype=F32), nr // s) for (_, nr, s) in segs])
    ang = pos[:, None] * inv_freq[None, :]
    cos, sin = jnp.cos(ang), jnp.sin(ang)
    zero = jnp.zeros_like(cos)
    zero2 = jnp.concatenate([zero, zero], axis=1)
    cos_t = jnp.concatenate([cos, cos, zero2], axis=1)
    sin_a = jnp.concatenate([-sin, zero, zero2], axis=1)
    sin_b = jnp.concatenate([zero, sin, zero2], axis=1)
    return cos_t, sin_a, sin_b


def mla_prep(pqkv, q_gain, kv_gain, w_uq, w_ukv, gq, gk, segs, tm=256):
    n = pqkv.shape[0]
    tm = _tile(n, tm)
    wq = jnp.pad(w_uq.reshape(MLA_Q_RANK, N_HEADS, MLA_QK),
                 ((0, 0), (0, 0), (0, MLA_QK_PAD - MLA_QK))).reshape(MLA_Q_RANK, -1).astype(BF16)
    wkv = w_ukv.astype(BF16)
    pad = lambda g: jnp.pad(g, (0, MLA_QK_PAD - MLA_QK)).reshape(1, MLA_QK_PAD)
    cos_t, sin_a, sin_b = _rope_tables(segs)
    const = lambda shape: pl.BlockSpec(shape, lambda i: (0, 0))
    tab = pl.BlockSpec((tm, LANES), lambda i: (i, 0))
    hd = lambda w: pl.BlockSpec((N_HEADS, tm, w), lambda i: (0, i, 0))
    return pl.pallas_call(
        _mla_prep_kernel,
        grid=(n // tm,),
        in_specs=[pl.BlockSpec((tm, MLA_Q_RANK), lambda i: (i, 0)),
                  pl.BlockSpec((tm, QKV_PAD - MLA_Q_RANK), lambda i: (i, 2)),
                  const(wq.shape), const(wkv.shape),
                  const((1, MLA_Q_RANK)), const((1, MLA_KV_RANK)),
                  const((1, MLA_QK_PAD)), const((1, MLA_QK_PAD)), tab, tab, tab],
        out_specs=[hd(MLA_QK_PAD), hd(MLA_QK_PAD), hd(HEAD_DIM)],
        out_shape=[jax.ShapeDtypeStruct((N_HEADS, n, MLA_QK_PAD), BF16),
                   jax.ShapeDtypeStruct((N_HEADS, n, MLA_QK_PAD), BF16),
                   jax.ShapeDtypeStruct((N_HEADS, n, HEAD_DIM), BF16)],
        compiler_params=_cparams(("parallel",)),
        name="mla_prep",
    )(pqkv, pqkv, wq, wkv, q_gain.reshape(1, -1), kv_gain.reshape(1, -1), pad(gq), pad(gk),
      cos_t, sin_a, sin_b)


def _attn_kernel(q_ref, k_ref, v_ref, o_ref):
    s = _dot_nt(q_ref[...], k_ref[...])
    m = jnp.max(s, axis=-1, keepdims=True)
    p = jnp.exp(s - m)
    l = jnp.sum(p, axis=-1, keepdims=True)
    o = _dot(p.astype(BF16), v_ref[...])
    o_ref[...] = (o / l).astype(o_ref.dtype)


def attention_segment(q, k, v, first_row, n_rows, seq_len, tq=512):
    tq = min(tq, seq_len)
    n_seq = n_rows // seq_len
    qb_per_seq = seq_len // tq
    qoff = first_row // tq
    soff = first_row // seq_len
    return pl.pallas_call(
        _attn_kernel,
        grid=(n_seq, N_HEADS, qb_per_seq),
        in_specs=[pl.BlockSpec((None, tq, MLA_QK_PAD), lambda b, h, i: (h, qoff + b * qb_per_seq + i, 0)),
                  pl.BlockSpec((None, seq_len, MLA_QK_PAD), lambda b, h, i: (h, soff + b, 0)),
                  pl.BlockSpec((None, seq_len, HEAD_DIM), lambda b, h, i: (h, soff + b, 0))],
        out_specs=pl.BlockSpec((tq, HEAD_DIM), lambda b, h, i: (b * qb_per_seq + i, h)),
        out_shape=jax.ShapeDtypeStruct((n_rows, BRANCH_WIDTH), BF16),
        compiler_params=_cparams(("parallel", "parallel", "parallel")),
        name="mla_attention",
    )(q, k, v)


def mla_mixer(pqkv, q_gain, kv_gain, w_uq, w_ukv, gq, gk, segs):
    q, k, v = mla_prep(pqkv, q_gain, kv_gain, w_uq, w_ukv, gq, gk, segs)
    outs = [attention_segment(q, k, v, f, nr, s) for (f, nr, s) in segs]
    return outs[0] if len(outs) == 1 else jnp.concatenate(outs, axis=0)


def _route_kernel(x_ref, w_ref, b_ref, o_ref):
    lg = _dot(x_ref[...], w_ref[...]) + b_ref[...]
    lane = lax.broadcasted_iota(jnp.int32, lg.shape, 1)
    big = jnp.int32(1 << 20)
    is_g = (lane >= N_EXPERTS) & (lane < N_EXPERTS + N_GROUPS)
    gl = jnp.where(is_g, lg, NEG_INF)
    gmax = jnp.max(gl, axis=-1, keepdims=True)
    gidx = jnp.min(jnp.where(is_g & (gl == gmax), lane - N_EXPERTS, big), axis=-1, keepdims=True)
    gsum = jnp.sum(jnp.where(is_g, jnp.exp(gl - gmax), 0.0), axis=-1, keepdims=True)
    g_top = 1.0 / gsum
    in_grp = (lane < N_EXPERTS) & ((lane >> 3) == gidx)
    el = jnp.where(in_grp, lg, NEG_INF)
    v1 = jnp.max(el, axis=-1, keepdims=True)
    i1 = jnp.min(jnp.where(in_grp & (el == v1), lane, big), axis=-1, keepdims=True)
    rest = in_grp & (lane != i1)
    el2 = jnp.where(rest, lg, NEG_INF)
    v2 = jnp.max(el2, axis=-1, keepdims=True)
    i2 = jnp.min(jnp.where(rest & (el2 == v2), lane, big), axis=-1, keepdims=True)
    d = jnp.exp(v2 - v1)
    p1 = 1.0 / (1.0 + d)
    p2 = d / (1.0 + d)
    out = jnp.where(lane == 0, i1.astype(F32),
                    jnp.where(lane == 1, i2.astype(F32),
                              jnp.where(lane == 2, g_top * p1,
                                        jnp.where(lane == 3, g_top * p2, 0.0))))
    o_ref[...] = out


def moe_route(xn, w_group, b_group, w_router, b_router, tm=512):
    n = xn.shape[0]
    tm = _tile(n, tm)
    padc = LANES - N_EXPERTS - N_GROUPS
    w = jnp.concatenate([w_router, w_group, jnp.zeros((D_MODEL, padc), F32)], axis=1).astype(BF16)
    b = jnp.concatenate([b_router, b_group, jnp.zeros((padc,), F32)]).reshape(1, LANES)
    r = pl.pallas_call(
        _route_kernel,
        grid=(n // tm,),
        in_specs=[pl.BlockSpec((tm, D_MODEL), lambda i: (i, 0)),
                  pl.BlockSpec((D_MODEL, LANES), lambda i: (0, 0)),
                  pl.BlockSpec((1, LANES), lambda i: (0, 0))],
        out_specs=pl.BlockSpec((tm, LANES), lambda i: (i, 0)),
        out_shape=jax.ShapeDtypeStruct((n, LANES), F32),
        compiler_params=_cparams(("parallel",)),
        name="moe_route",
    )(xn, w, b)
    return r[:, :TOP_K].astype(jnp.int32), r[:, TOP_K:2 * TOP_K]


def _dispatch_plan(eid, gate, tm):
    n = eid.shape[0]
    m = n * TOP_K
    flat_e = eid.reshape(-1)
    onehot = (flat_e[:, None] == jnp.arange(N_EXPERTS, dtype=jnp.int32)[None, :]).astype(jnp.int32)
    csum = jnp.cumsum(onehot, axis=0)
    rank = jnp.take_along_axis(csum, flat_e[:, None], axis=1)[:, 0] - 1
    counts = csum[-1]
    padded = (counts + tm - 1) // tm * tm
    pad_end = jnp.cumsum(padded)
    pad_start = pad_end - padded
    dest = pad_start[flat_e] + rank
    p = (m + tm - 1) // tm * tm + N_EXPERTS * tm
    tok_buf = jnp.zeros((p,), jnp.int32).at[dest].set(jnp.arange(m, dtype=jnp.int32) // TOP_K)
    w_buf = jnp.zeros((p,), F32).at[dest].set(gate.reshape(-1))
    n_blocks = p // tm
    block_e = jnp.minimum(jnp.searchsorted(pad_end, jnp.arange(n_blocks, dtype=jnp.int32) * tm, side="right"),
                          N_EXPERTS - 1).astype(jnp.int32)
    n_used = (pad_end[-1] // tm).astype(jnp.int32).reshape(1)
    return tok_buf, w_buf, block_e, n_used, dest.reshape(n, TOP_K).astype(jnp.int32)


def _gather_rows_kernel(tok_ref, src_hbm, dst_hbm, sem, *, rows):
    i = pl.program_id(0)

    def row_copy(r):
        return pltpu.make_async_copy(src_hbm.at[tok_ref[0, 0, r]], dst_hbm.at[i * rows + r], sem)

    def start(r, carry):
        row_copy(r).start()
        return carry

    def wait(r, carry):
        row_copy(r).wait()
        return carry

    lax.fori_loop(0, rows, start, 0)
    lax.fori_loop(0, rows, wait, 0)


def moe_gather(xn, tok_buf, tm):
    n = xn.shape[0]
    p = tok_buf.shape[0]
    nb = p // tm
    out = pl.pallas_call(
        functools.partial(_gather_rows_kernel, rows=tm),
        grid=(nb,),
        in_specs=[pl.BlockSpec((1, 1, tm), lambda i: (i, 0, 0), memory_space=pltpu.SMEM),
                  pl.BlockSpec(memory_space=pl.ANY)],
        out_specs=pl.BlockSpec(memory_space=pl.ANY),
        scratch_shapes=[pltpu.SemaphoreType.DMA(())],
        out_shape=jax.ShapeDtypeStruct((p, ROW_WORDS, LANES), xn.dtype),
        compiler_params=_cparams(("arbitrary",)),
        name="moe_gather",
    )(tok_buf.reshape(nb, 1, tm), xn.reshape(n, ROW_WORDS, LANES))
    return out.reshape(p, D_MODEL)


def _moe_up_kernel(be_ref, nu_ref, xs_ref, wg_ref, wu_ref, h_ref, wg_bf, wu_bf):
    i = pl.program_id(1)

    @pl.when(i < nu_ref[0])
    def _():
        changed = (i == 0) | (be_ref[jnp.maximum(i - 1, 0)] != be_ref[i])

        @pl.when(changed)
        def _():
            wg_bf[...] = wg_ref[...].astype(BF16)
            wu_bf[...] = wu_ref[...].astype(BF16)

        x = xs_ref[...]
        g = _dot(x, wg_bf[...])
        u = _dot(x, wu_bf[...])
        h_ref[...] = (jax.nn.silu(g) * u).astype(h_ref.dtype)

    @pl.when(i >= nu_ref[0])
    def _():
        h_ref[...] = jnp.zeros_like(h_ref)


def moe_up(xs, w_gate, w_up, layer, block_e, n_used, tm, tn=256):
    p = xs.shape[0]
    nb = p // tm
    clamp = lambda i, nu: jnp.minimum(i, nu[0] - 1)
    w_spec = pl.BlockSpec((None, None, D_MODEL, tn), lambda j, i, be, nu: (layer, be[clamp(i, nu)], 0, j))
    return pl.pallas_call(
        _moe_up_kernel,
        grid_spec=pltpu.PrefetchScalarGridSpec(
            num_scalar_prefetch=2,
            grid=(EXPERT_FF // tn, nb),
            in_specs=[pl.BlockSpec((tm, D_MODEL), lambda j, i, be, nu: (clamp(i, nu), 0)), w_spec, w_spec],
            out_specs=pl.BlockSpec((tm, tn), lambda j, i, be, nu: (i, j)),
            scratch_shapes=[pltpu.VMEM((D_MODEL, tn), BF16), pltpu.VMEM((D_MODEL, tn), BF16)]),
        out_shape=jax.ShapeDtypeStruct((p, EXPERT_FF), BF16),
        compiler_params=_cparams(("arbitrary", "arbitrary")),
        name="moe_up",
    )(block_e, n_used, xs, w_gate, w_up)


def _moe_down_kernel(be_ref, nu_ref, h_ref, wd_ref, w_ref, y_ref, wd_bf):
    i = pl.program_id(1)

    @pl.when(i < nu_ref[0])
    def _():
        changed = (i == 0) | (be_ref[jnp.maximum(i - 1, 0)] != be_ref[i])

        @pl.when(changed)
        def _():
            wd_bf[...] = wd_ref[...].astype(BF16)

        y_ref[...] = _dot(h_ref[...], wd_bf[...]) * w_ref[...]

    @pl.when(i >= nu_ref[0])
    def _():
        y_ref[...] = jnp.zeros_like(y_ref)


def moe_down(h, w_down, layer, w_buf, block_e, n_used, tm, tn=1024):
    p = h.shape[0]
    nb = p // tm
    clamp = lambda i, nu: jnp.minimum(i, nu[0] - 1)
    return pl.pallas_call(
        _moe_down_kernel,
        grid_spec=pltpu.PrefetchScalarGridSpec(
            num_scalar_prefetch=2,
            grid=(D_MODEL // tn, nb),
            in_specs=[pl.BlockSpec((tm, EXPERT_FF), lambda j, i, be, nu: (clamp(i, nu), 0)),
                      pl.BlockSpec((None, None, EXPERT_FF, tn),
                                   lambda j, i, be, nu: (layer, be[clamp(i, nu)], 0, j)),
                      pl.BlockSpec((tm, 1), lambda j, i, be, nu: (clamp(i, nu), 0))],
            out_specs=pl.BlockSpec((tm, tn), lambda j, i, be, nu: (i, j)),
            scratch_shapes=[pltpu.VMEM((EXPERT_FF, tn), BF16)]),
        out_shape=jax.ShapeDtypeStruct((p, D_MODEL), F32),
        compiler_params=_cparams(("arbitrary", "arbitrary")),
        name="moe_down",
    )(block_e, n_used, h, w_down, w_buf.reshape(p, 1))


def _combine_kernel(pos_ref, x_ref, ys_hbm, o_ref, buf, sem, *, tc):
    def row_copy(r):
        return pltpu.make_async_copy(ys_hbm.at[pos_ref[0, 0, r]], buf.at[r], sem)

    def start(r, carry):
        row_copy(r).start()
        return carry

    def wait(r, carry):
        row_copy(r).wait()
        return carry

    lax.fori_loop(0, TOP_K * tc, start, 0)
    lax.fori_loop(0, TOP_K * tc, wait, 0)
    o_ref[...] = x_ref[...] + buf[0:tc] + buf[tc:2 * tc]


def moe_combine(x, ys, pos, tc=128):
    n = x.shape[0]
    p = ys.shape[0]
    tc = _tile(n, tc)
    nb = n // tc
    pos_blk = jnp.swapaxes(pos.reshape(nb, tc, TOP_K), 1, 2).reshape(nb, 1, TOP_K * tc)
    blk = pl.BlockSpec((tc, ROW_WORDS, LANES), lambda i: (i, 0, 0))
    out = pl.pallas_call(
        functools.partial(_combine_kernel, tc=tc),
        grid=(nb,),
        in_specs=[pl.BlockSpec((1, 1, TOP_K * tc), lambda i: (i, 0, 0), memory_space=pltpu.SMEM),
                  blk, pl.BlockSpec(memory_space=pl.ANY)],
        out_specs=blk,
        out_shape=jax.ShapeDtypeStruct((n, ROW_WORDS, LANES), F32),
        scratch_shapes=[pltpu.VMEM((TOP_K * tc, ROW_WORDS, LANES), F32), pltpu.SemaphoreType.DMA(())],
        compiler_params=_cparams(("arbitrary",)),
        name="moe_combine",
    )(pos_blk, x.reshape(n, ROW_WORDS, LANES), ys.reshape(p, ROW_WORDS, LANES))
    return out.reshape(n, D_MODEL)


def hier_moe_residual(x, xn, layer, w_group, b_group, w_router, b_router, w_gate, w_up, w_down,
                      tm=MOE_BLOCK):
    eid, gate = moe_route(xn, w_group, b_group, w_router, b_router)
    tok_buf, w_buf, block_e, n_used, pos = _dispatch_plan(eid, gate, tm)
    xs = moe_gather(xn, tok_buf, tm)
    h = moe_up(xs, w_gate, w_up, layer, block_e, n_used, tm)
    ys = moe_down(h, w_down, layer, w_buf, block_e, n_used, tm)
    return moe_combine(x, ys, pos)


def encoder_layer(x, layer, segs, lb_f, lb_b, norm_mix, w_in, gmlp_v_norm, gmlp_w_s, gmlp_b_s,
                  hgrn_out_norm, conv_w, mla_q_norm, mla_kv_norm, mla_w_uq, mla_w_ukv, qk_norm_q,
                  qk_norm_k, w_branch, w_out, norm_ffn, w_group, b_group, w_router, b_router,
                  w_gate, w_up, w_down):
    w_in_l = w_in[layer]
    w_abc = w_in_l[:, :ABC_WIDTH].astype(BF16)
    w_qkv = jnp.pad(w_in_l[:, ABC_WIDTH:ABC_WIDTH + QKV_WIDTH], ((0, 0), (0, QKV_PAD - QKV_WIDTH))).astype(BF16)
    w_gl = w_in_l[:, ABC_WIDTH + QKV_WIDTH:].astype(BF16)

    xn = rmsnorm_rows(x, norm_mix[layer])
    pabc = matmul(xn, w_abc, out_dtype=F32, tm=1024, tn=512, name="proj_abc")
    pqkv = matmul(xn, w_qkv, out_dtype=F32, tm=1024, tn=384, name="proj_qkv")
    gates = matmul(xn, w_gl, out_dtype=BF16, tm=1024, tn=512, act="sigmoid", name="proj_gates")

    branches = (
        gmlp_mixer(pabc, gmlp_v_norm[layer], gmlp_w_s[layer], gmlp_b_s[layer]),
        hgrn_mixer(pabc, lb_f, lb_b, hgrn_out_norm[layer], segs),
        conv_mixer(pabc, conv_w[layer], segs),
        mla_mixer(pqkv, mla_q_norm[layer], mla_kv_norm[layer], mla_w_uq[layer], mla_w_ukv[layer],
                  qk_norm_q[layer], qk_norm_k[layer], segs),
    )
    merged = merge_branches(branches, w_branch[layer].astype(BF16), gates)
    x = matmul(merged, w_out[layer].astype(BF16), out_dtype=F32, tm=1024, tn=512, res=x, name="out_proj")
    xn2 = rmsnorm_rows(x, norm_ffn[layer])
    return hier_moe_residual(x, xn2, layer, w_group[layer], b_group[layer], w_router[layer],
                             b_router[layer], w_gate, w_up, w_down)


def kernel(x_prompt, x_sample, norm_mix, w_in, gmlp_v_norm, gmlp_w_s, gmlp_b_s, hgrn_lb_logits, hgrn_out_norm, conv_w, mla_q_norm, mla_kv_norm, mla_w_uq, mla_w_ukv, qk_norm_q, qk_norm_k, w_branch, w_out, norm_ffn, w_group, b_group, w_router, b_router, w_gate, w_up, w_down):
    depth = norm_mix.shape[0]
    bp, sp, d = x_prompt.shape
    bs, ss, _ = x_sample.shape
    n_p, n_s = bp * sp, bs * ss
    segs = ((0, n_p, sp), (n_p, n_s, ss))
    x = jnp.concatenate([x_prompt.reshape(n_p, d), x_sample.reshape(n_s, d)], axis=0)

    lb_p = jax.nn.softmax(hgrn_lb_logits.astype(F32), axis=1)
    lb = jnp.maximum(jnp.cumsum(lb_p, axis=1) - lb_p[:, :1], 0.0)

    for layer in range(depth):
        x = encoder_layer(x, layer, segs, lb[0, layer], lb[1, layer], norm_mix, w_in, gmlp_v_norm, gmlp_w_s,
                          gmlp_b_s, hgrn_out_norm, conv_w, mla_q_norm, mla_kv_norm, mla_w_uq, mla_w_ukv,
                          qk_norm_q, qk_norm_k, w_branch, w_out, norm_ffn, w_group, b_group, w_router,
                          b_router, w_gate, w_up, w_down)
    return (x[:n_p].reshape(bp, sp, d), x[n_p:].reshape(bs, ss, d))
```

```python
import functools
import math

import jax
import jax.numpy as jnp
import numpy as np
from jax import lax
from jax.experimental import pallas as pl
from jax.experimental.pallas import tpu as pltpu

F32 = jnp.float32
BF16 = jnp.bfloat16

D_MODEL = 4096
BRANCH_WIDTH = 1024
HEAD_DIM = 128
N_HEADS = 8
GMLP_CHUNK = 128
MLA_Q_RANK = 768
MLA_KV_RANK = 256
MLA_NOPE = 128
MLA_ROPE = 64
MLA_QK = MLA_NOPE + MLA_ROPE
MLA_QK_PAD = 256
ROPE_BASE = 10000.0
N_GROUPS = 8
EXPERTS_PER_GROUP = 8
N_EXPERTS = 64
TOP_K = 2
EXPERT_FF = 1024
EPS = 1e-6
ABC_WIDTH = 10 * BRANCH_WIDTH
QKV_WIDTH = MLA_Q_RANK + MLA_KV_RANK + MLA_ROPE
QKV_PAD = 1152
LANES = 128
VMEM_LIMIT = 56 * 1024 * 1024
HGRN_CHUNK = 64
MOE_BLOCK = 256
NEG_INF = float("-inf")


def _cparams(sem):
    return pltpu.CompilerParams(dimension_semantics=sem, vmem_limit_bytes=VMEM_LIMIT)


def _tile(n, t):
    t = min(t, n)
    while n % t:
        t //= 2
    return t


def _dot(a, b):
    return jnp.dot(a, b, preferred_element_type=F32)


def _dot_nt(a, b):
    return lax.dot_general(a, b, (((1,), (1,)), ((), ())), preferred_element_type=F32)


def _rmsnorm_kernel(x_ref, g_ref, o_ref):
    x = x_ref[...]
    ms = jnp.mean(x * x, axis=-1, keepdims=True)
    o_ref[...] = (x * lax.rsqrt(ms + EPS) * g_ref[...]).astype(o_ref.dtype)


def rmsnorm_rows(x, g, tm=512):
    n, d = x.shape
    tm = _tile(n, tm)
    return pl.pallas_call(
        _rmsnorm_kernel,
        grid=(n // tm,),
        in_specs=[pl.BlockSpec((tm, d), lambda i: (i, 0)),
                  pl.BlockSpec((1, d), lambda i: (0, 0))],
        out_specs=pl.BlockSpec((tm, d), lambda i: (i, 0)),
        out_shape=jax.ShapeDtypeStruct((n, d), BF16),
        compiler_params=_cparams(("parallel",)),
        name="rmsnorm",
    )(x, g.reshape(1, d))


def _mm_kernel(x_ref, w_ref, *rest, act, has_res):
    o_ref = rest[-1]
    acc = _dot(x_ref[...], w_ref[...])
    if act == "sigmoid":
        acc = jax.nn.sigmoid(acc)
    if has_res:
        acc = acc + rest[0][...]
    o_ref[...] = acc.astype(o_ref.dtype)


def matmul(x, w, *, out_dtype, tm, tn, act=None, res=None, name="matmul"):
    m, k = x.shape
    n = w.shape[1]
    tm = _tile(m, tm)
    in_specs = [pl.BlockSpec((tm, k), lambda i, j: (i, 0)),
                pl.BlockSpec((k, tn), lambda i, j: (0, j))]
    args = [x, w]
    if res is not None:
        in_specs.append(pl.BlockSpec((tm, tn), lambda i, j: (i, j)))
        args.append(res)
    return pl.pallas_call(
        functools.partial(_mm_kernel, act=act, has_res=res is not None),
        grid=(m // tm, n // tn),
        in_specs=in_specs,
        out_specs=pl.BlockSpec((tm, tn), lambda i, j: (i, j)),
        out_shape=jax.ShapeDtypeStruct((m, n), out_dtype),
        compiler_params=_cparams(("parallel", "parallel")),
        name=name,
    )(*args)


def _merge_kernel(b0, b1, b2, b3, wb_ref, g0, g1, g2, g3, o_ref):
    acc = None
    for kb, (b_ref, g_ref) in enumerate(((b0, g0), (b1, g1), (b2, g2), (b3, g3))):
        t = _dot(b_ref[...], wb_ref[kb]) * g_ref[...].astype(F32)
        acc = t if acc is None else acc + t
    o_ref[...] = acc.astype(o_ref.dtype)


def merge_branches(branches, w_branch, gates, tm=1024, tn=512):
    n = branches[0].shape[0]
    tm = _tile(n, tm)
    nj = D_MODEL // tn
    b_spec = pl.BlockSpec((tm, BRANCH_WIDTH), lambda i, j: (i, 0))
    g_specs = [pl.BlockSpec((tm, tn), functools.partial(lambda i, j, kb: (i, kb * nj + j), kb=kb))
               for kb in range(4)]
    return pl.pallas_call(
        _merge_kernel,
        grid=(n // tm, nj),
        in_specs=[b_spec] * 4 + [pl.BlockSpec((4, BRANCH_WIDTH, tn), lambda i, j: (0, 0, j))] + g_specs,
        out_specs=pl.BlockSpec((tm, tn), lambda i, j: (i, j)),
        out_shape=jax.ShapeDtypeStruct((n, D_MODEL), BF16),
        compiler_params=_cparams(("parallel", "parallel")),
        name="merge_branches",
    )(*branches, w_branch, gates, gates, gates, gates)


def _gmlp_kernel(u_ref, v_ref, g_ref, ws_ref, bs_ref, o_ref, *, tm):
    for c in range(tm // GMLP_CHUNK):
        rows = pl.ds(c * GMLP_CHUNK, GMLP_CHUNK)
        u = jax.nn.gelu(u_ref[rows, :])
        v = jax.nn.gelu(v_ref[rows, :])
        ms = jnp.mean(v * v, axis=-1, keepdims=True)
        vb = (v * lax.rsqrt(ms + EPS) * g_ref[...]).astype(BF16)
        for h in range(N_HEADS):
            cols = slice(h * HEAD_DIM, (h + 1) * HEAD_DIM)
            mixed = _dot(ws_ref[h], vb[:, cols]) + bs_ref[:, cols]
            o_ref[rows, cols] = (u[:, cols] * mixed).astype(o_ref.dtype)


def gmlp_mixer(pabc, v_gain, w_s, b_s, tm=512):
    n = pabc.shape[0]
    tm = _tile(n, tm)
    bias = jnp.repeat(jnp.swapaxes(b_s, 0, 1), HEAD_DIM, axis=1)
    return pl.pallas_call(
        functools.partial(_gmlp_kernel, tm=tm),
        grid=(n // tm,),
        in_specs=[pl.BlockSpec((tm, BRANCH_WIDTH), lambda i: (i, 0)),
                  pl.BlockSpec((tm, BRANCH_WIDTH), lambda i: (i, 1)),
                  pl.BlockSpec((1, BRANCH_WIDTH), lambda i: (0, 0)),
                  pl.BlockSpec((N_HEADS, GMLP_CHUNK, GMLP_CHUNK), lambda i: (0, 0, 0)),
                  pl.BlockSpec((GMLP_CHUNK, BRANCH_WIDTH), lambda i: (0, 0))],
        out_specs=pl.BlockSpec((tm, BRANCH_WIDTH), lambda i: (i, 0)),
        out_shape=jax.ShapeDtypeStruct((n, BRANCH_WIDTH), BF16),
        compiler_params=_cparams(("parallel",)),
        name="gmlp_mixer",
    )(pabc, pabc, v_gain.reshape(1, -1), w_s.astype(BF16), bias)


def _segment_of(row0, segs):
    seq_len = jnp.int32(segs[-1][2])
    first = jnp.int32(segs[-1][0])
    for (f, nr, s) in reversed(segs[:-1]):
        inside = row0 < f + nr
        seq_len = jnp.where(inside, s, seq_len)
        first = jnp.where(inside, f, first)
    return seq_len, first


def _conv_kernel(bg_ref, cg_ref, h_ref, cgp_ref, hp_ref, cgn_ref, hn_ref, w_ref, o_ref, *, tm, segs):
    row0 = pl.program_id(0) * tm
    seq_len, first = _segment_of(row0, segs)
    at_start = lax.rem(row0 - first, seq_len) == 0
    at_end = lax.rem(row0 - first + tm, seq_len) == 0
    z = cg_ref[...] * h_ref[...]
    left = jnp.where(at_start, 0.0, cgp_ref[7:8, :] * hp_ref[7:8, :])
    right = jnp.where(at_end, 0.0, cgn_ref[0:1, :] * hn_ref[0:1, :])
    ridx = lax.broadcasted_iota(jnp.int32, z.shape, 0)
    z_prev = jnp.where(ridx == 0, left, pltpu.roll(z, 1, axis=0))
    z_next = jnp.where(ridx == tm - 1, right, pltpu.roll(z, tm - 1, axis=0))
    y = w_ref[0:1, :] * z_prev + w_ref[1:2, :] * z + w_ref[2:3, :] * z_next
    o_ref[...] = (bg_ref[...] * y).astype(o_ref.dtype)


def conv_mixer(pabc, conv_w, segs, tm=256):
    n = pabc.shape[0]
    tm = _tile(n, tm)
    nb8 = n // 8
    blk = lambda c: pl.BlockSpec((tm, BRANCH_WIDTH), lambda i: (i, c))
    prev8 = lambda c: pl.BlockSpec((8, BRANCH_WIDTH), lambda i: (jnp.maximum(i * (tm // 8) - 1, 0), c))
    next8 = lambda c: pl.BlockSpec((8, BRANCH_WIDTH), lambda i: (jnp.minimum((i + 1) * (tm // 8), nb8 - 1), c))
    return pl.pallas_call(
        functools.partial(_conv_kernel, tm=tm, segs=segs),
        grid=(n // tm,),
        in_specs=[blk(7), blk(8), blk(9), prev8(8), prev8(9), next8(8), next8(9),
                  pl.BlockSpec((3, BRANCH_WIDTH), lambda i: (0, 0))],
        out_specs=pl.BlockSpec((tm, BRANCH_WIDTH), lambda i: (i, 0)),
        out_shape=jax.ShapeDtypeStruct((n, BRANCH_WIDTH), BF16),
        compiler_params=_cparams(("parallel",)),
        name="conv_mixer",
    )(pabc, pabc, pabc, pabc, pabc, pabc, pabc, conv_w)


def _hgrn_constants(c, reverse):
    n_lev = int(math.log2(c))
    t = np.arange(c)[:, None]
    u = np.arange(c)[None, :]
    if not reverse:
        w_a = u <= t
        w_kd = u > t
    else:
        w_a = u >= t
        w_kd = u < t
    ws = [w_a, w_kd]
    masks = [np.eye(c, dtype=bool)]
    for lev in range(n_lev):
        m = c >> (lev + 1)
        blk_t = t // (2 * m)
        upper_t = (t % (2 * m)) >= m
        p_last_lower = blk_t * 2 * m + m - 1
        if not reverse:
            w_q = (u > p_last_lower) & (u <= t)
            w_k = (u > t) & (u <= p_last_lower)
            w = np.where(upper_t, w_q, w_k)
            query_t = upper_t
        else:
            p_first_upper = p_last_lower + 1
            w_q = (u >= t) & (u < p_first_upper)
            w_k = (u >= p_first_upper) & (u < t)
            w = np.where(upper_t, w_k, w_q)
            query_t = ~upper_t
        ws.append(w)
        same_blk = blk_t == blk_t.T
        masks.append(same_blk & query_t & (~query_t).T)
    w_all = np.concatenate(ws, axis=0).astype(np.float32)
    return jnp.asarray(w_all, BF16), jnp.asarray(np.stack(masks), F32)


def _log_forget_and_key(z, log_lb, log1m_lb):
    log_sig = jnp.minimum(z, 0.0) - jnp.log1p(jnp.exp(-jnp.abs(z)))
    x2 = log1m_lb + log_sig
    hi = jnp.maximum(log_lb, x2)
    lf = hi + jnp.log1p(jnp.exp(-jnp.abs(log_lb - x2)))
    return lf, jnp.exp(x2 - z)


def _split3(x):
    p1 = x.astype(BF16)
    r1 = x - p1.astype(F32)
    p2 = r1.astype(BF16)
    p3 = (r1 - p2.astype(F32)).astype(BF16)
    return p1, p2, p3


def _hgrn_kernel(q_ref, v_ref, z_ref, *rest, c, n_lev, reverse, segs, final):
    if final:
        g_ref, of_ref, gain_ref = rest[:3]
        rest = rest[3:]
    lb_ref, wall_ref, mask_ref, o_ref, st_ref = rest
    step = pl.program_id(0)
    n_steps = pl.num_programs(0)
    chunk = (n_steps - 1 - step) if reverse else step
    row0 = chunk * c
    seq_len, first = _segment_of(row0, segs)
    if reverse:
        fresh = lax.rem(row0 - first + c, seq_len) == 0
    else:
        fresh = lax.rem(row0 - first, seq_len) == 0

    @pl.when(fresh)
    def _():
        st_ref[...] = jnp.zeros_like(st_ref)

    w_all = wall_ref[...]
    q = q_ref[...]
    v = v_ref[...]
    lf, k = _log_forget_and_key(z_ref[...], lb_ref[0:1, :], lb_ref[1:2, :])
    p1, p2, p3 = _split3(lf)
    e_all = _dot(w_all, p1) + _dot(w_all, p2) + _dot(w_all, p3)
    a = e_all[0:c]
    qd = (q * jnp.exp(a)).astype(BF16)
    kd = (k * jnp.exp(e_all[c:2 * c])).astype(BF16)
    qb = q.astype(BF16)
    kb = k.astype(BF16)
    vb = v.astype(BF16)
    row = lax.broadcasted_iota(jnp.int32, q.shape, 0)
    xms = []
    for lev in range(n_lev):
        half_bit = (row >> (n_lev - 1 - lev)) & 1
        is_query = (half_bit == 0) if reverse else (half_bit == 1)
        dm = jnp.exp(e_all[(2 + lev) * c:(3 + lev) * c])
        xms.append((jnp.where(is_query, q, k) * dm).astype(BF16))
    decay_end = jnp.exp(a[0:1] if reverse else a[c - 1:c])
    for h in range(N_HEADS):
        cols = slice(h * HEAD_DIM, (h + 1) * HEAD_DIM)
        scores = mask_ref[0] * _dot_nt(qb[:, cols], kb[:, cols])
        for lev in range(n_lev):
            xm = xms[lev][:, cols]
            scores = scores + mask_ref[lev + 1] * _dot_nt(xm, xm)
        st = st_ref[h]
        o = _dot_nt(qd[:, cols], st.astype(BF16)) + _dot(scores.astype(BF16), vb[:, cols])
        st_ref[h] = st * decay_end[:, cols] + _dot(v[:, cols].T.astype(BF16), kd[:, cols])
        if final:
            o = o + of_ref[:, cols]
            ms = jnp.mean(o * o, axis=-1, keepdims=True)
            o = o * lax.rsqrt(ms + EPS) * gain_ref[...] * jax.nn.silu(g_ref[:, cols])
        o_ref[:, cols] = o.astype(o_ref.dtype)


def _hgrn_pass(pabc, z_col, log_lb2, segs, reverse, extra, c=HGRN_CHUNK):
    n = pabc.shape[0]
    n_lev = int(math.log2(c))
    n_chunks = n // c
    w_all, masks = _hgrn_constants(c, reverse)
    if reverse:
        row = lambda col: pl.BlockSpec((c, BRANCH_WIDTH), lambda i: (n_chunks - 1 - i, col))
    else:
        row = lambda col: pl.BlockSpec((c, BRANCH_WIDTH), lambda i: (i, col))
    const2 = lambda shape: pl.BlockSpec(shape, lambda i: (0, 0))
    const3 = lambda shape: pl.BlockSpec(shape, lambda i: (0, 0, 0))
    in_specs = [row(2), row(3), row(z_col)]
    args = [pabc, pabc, pabc]
    final = extra is not None
    if final:
        o_fwd, out_gain = extra
        in_specs += [row(6), row(0), const2((1, HEAD_DIM))]
        args += [pabc, o_fwd, out_gain.reshape(1, HEAD_DIM)]
    in_specs += [const2((2, BRANCH_WIDTH)), const2(w_all.shape), const3(masks.shape)]
    args += [log_lb2, w_all, masks]
    return pl.pallas_call(
        functools.partial(_hgrn_kernel, c=c, n_lev=n_lev, reverse=reverse, segs=segs, final=final),
        grid=(n_chunks,),
        in_specs=in_specs,
        out_specs=row(0),
        out_shape=jax.ShapeDtypeStruct((n, BRANCH_WIDTH), BF16 if final else F32),
        scratch_shapes=[pltpu.VMEM((N_HEADS, HEAD_DIM, HEAD_DIM), F32)],
        compiler_params=_cparams(("arbitrary",)),
        name="hgrn_bwd" if reverse else "hgrn_fwd",
    )(*args)


def hgrn_mixer(pabc, lb_f, lb_b, out_gain, segs):
    lbs_f = jnp.stack([jnp.log(lb_f), jnp.log1p(-lb_f)])
    lbs_b = jnp.stack([jnp.log(lb_b), jnp.log1p(-lb_b)])
    o_fwd = _hgrn_pass(pabc, 4, lbs_f, segs, False, None)
    return _hgrn_pass(pabc, 5, lbs_b, segs, True, (o_fwd, out_gain))


def _rope_tile(r, cos_t, sin_a, sin_b):
    return r * cos_t + pltpu.roll(r, 96, axis=1) * sin_a + pltpu.roll(r, 32, axis=1) * sin_b


def _mla_prep_kernel(cq_ref, ckv_ref, wq_ref, wkv_ref, qg_ref, kvg_ref, gq_ref, gk_ref,
                     cos_ref, sina_ref, sinb_ref, q_out, k_out, v_out):
    cq = cq_ref[...]
    hq = (cq * lax.rsqrt(jnp.mean(cq * cq, axis=-1, keepdims=True) + EPS) * qg_ref[...]).astype(BF16)
    qf = _dot(hq, wq_ref[...])
    ckvkr = ckv_ref[...]
    ckv = ckvkr[:, :MLA_KV_RANK]
    kr = ckvkr[:, MLA_KV_RANK:]
    hk = (ckv * lax.rsqrt(jnp.mean(ckv * ckv, axis=-1, keepdims=True) + EPS) * kvg_ref[...]).astype(BF16)
    kvf = _dot(hk, wkv_ref[...])
    cos_t, sin_a, sin_b = cos_ref[...], sina_ref[...], sinb_ref[...]
    gq_n, gq_r = gq_ref[:, :MLA_NOPE], gq_ref[:, MLA_NOPE:]
    gk_n, gk_r = gk_ref[:, :MLA_NOPE], gk_ref[:, MLA_NOPE:]
    kr_ss = jnp.sum(kr * kr, axis=-1, keepdims=True)
    kr_rot = _rope_tile(kr * gk_r, cos_t, sin_a, sin_b)
    scale = MLA_QK ** -0.5
    for h in range(N_HEADS):
        base = h * MLA_QK_PAD
        qn = qf[:, base:base + MLA_NOPE]
        qr = qf[:, base + MLA_NOPE:base + MLA_QK_PAD]
        ss = jnp.sum(qn * qn, axis=-1, keepdims=True) + jnp.sum(qr * qr, axis=-1, keepdims=True)
        rstd = lax.rsqrt(ss * (1.0 / MLA_QK) + EPS) * scale
        q_out[h, :, :MLA_NOPE] = (qn * rstd * gq_n).astype(BF16)
        q_out[h, :, MLA_NOPE:] = _rope_tile(qr * rstd * gq_r, cos_t, sin_a, sin_b).astype(BF16)
        kn = kvf[:, base:base + MLA_NOPE]
        ssk = jnp.sum(kn * kn, axis=-1, keepdims=True) + kr_ss
        rstdk = lax.rsqrt(ssk * (1.0 / MLA_QK) + EPS)
        k_out[h, :, :MLA_NOPE] = (kn * rstdk * gk_n).astype(BF16)
        k_out[h, :, MLA_NOPE:] = (kr_rot * rstdk).astype(BF16)
        v_out[h] = kvf[:, base + MLA_NOPE:base + MLA_QK_PAD].astype(BF16)


def _rope_tables(segs):
    half = MLA_ROPE // 2
    inv_freq = ROPE_BASE ** (-jnp.arange(half, dtype=F32) * 2.0 / MLA_ROPE)
    pos = jnp.concatenate([jnp.tile(jnp.arange(s, dtype=F32), nr // s) for (_, nr, s) in segs])
    ang = pos[:, None] * inv_freq[None, :]
    cos, sin = jnp.cos(ang), jnp.sin(ang)
    zero = jnp.zeros_like(cos)
    zero2 = jnp.concatenate([zero, zero], axis=1)
    cos_t = jnp.concatenate([cos, cos, zero2], axis=1)
    sin_a = jnp.concatenate([-sin, zero, zero2], axis=1)
    sin_b = jnp.concatenate([zero, sin, zero2], axis=1)
    return cos_t, sin_a, sin_b


def mla_prep(pqkv, q_gain, kv_gain, w_uq, w_ukv, gq, gk, segs, tm=256):
    n = pqkv.shape[0]
    tm = _tile(n, tm)
    wq = jnp.pad(w_uq.reshape(MLA_Q_RANK, N_HEADS, MLA_QK),
                 ((0, 0), (0, 0), (0, MLA_QK_PAD - MLA_QK))).reshape(MLA_Q_RANK, -1).astype(BF16)
    wkv = w_ukv.astype(BF16)
    pad = lambda g: jnp.pad(g, (0, MLA_QK_PAD - MLA_QK)).reshape(1, MLA_QK_PAD)
    cos_t, sin_a, sin_b = _rope_tables(segs)
    const = lambda shape: pl.BlockSpec(shape, lambda i: (0, 0))
    tab = pl.BlockSpec((tm, LANES), lambda i: (i, 0))
    hd = lambda w: pl.BlockSpec((N_HEADS, tm, w), lambda i: (0, i, 0))
    return pl.pallas_call(
        _mla_prep_kernel,
        grid=(n // tm,),
        in_specs=[pl.BlockSpec((tm, MLA_Q_RANK), lambda i: (i, 0)),
                  pl.BlockSpec((tm, QKV_PAD - MLA_Q_RANK), lambda i: (i, 2)),
                  const(wq.shape), const(wkv.shape),
                  const((1, MLA_Q_RANK)), const((1, MLA_KV_RANK)),
                  const((1, MLA_QK_PAD)), const((1, MLA_QK_PAD)), tab, tab, tab],
        out_specs=[hd(MLA_QK_PAD), hd(MLA_QK_PAD), hd(HEAD_DIM)],
        out_shape=[jax.ShapeDtypeStruct((N_HEADS, n, MLA_QK_PAD), BF16),
                   jax.ShapeDtypeStruct((N_HEADS, n, MLA_QK_PAD), BF16),
                   jax.ShapeDtypeStruct((N_HEADS, n, HEAD_DIM), BF16)],
        compiler_params=_cparams(("parallel",)),
        name="mla_prep",
    )(pqkv, pqkv, wq, wkv, q_gain.reshape(1, -1), kv_gain.reshape(1, -1), pad(gq), pad(gk),
      cos_t, sin_a, sin_b)


def _attn_kernel(q_ref, k_ref, v_ref, o_ref):
    s = _dot_nt(q_ref[...], k_ref[...])
    m = jnp.max(s, axis=-1, keepdims=True)
    p = jnp.exp(s - m)
    l = jnp.sum(p, axis=-1, keepdims=True)
    o = _dot(p.astype(BF16), v_ref[...])
    o_ref[...] = (o / l).astype(o_ref.dtype)


def attention_segment(q, k, v, first_row, n_rows, seq_len, tq=512):
    tq = min(tq, seq_len)
    n_seq = n_rows // seq_len
    qb_per_seq = seq_len // tq
    qoff = first_row // tq
    soff = first_row // seq_len
    return pl.pallas_call(
        _attn_kernel,
        grid=(n_seq, N_HEADS, qb_per_seq),
        in_specs=[pl.BlockSpec((None, tq, MLA_QK_PAD), lambda b, h, i: (h, qoff + b * qb_per_seq + i, 0)),
                  pl.BlockSpec((None, seq_len, MLA_QK_PAD), lambda b, h, i: (h, soff + b, 0)),
                  pl.BlockSpec((None, seq_len, HEAD_DIM), lambda b, h, i: (h, soff + b, 0))],
        out_specs=pl.BlockSpec((tq, HEAD_DIM), lambda b, h, i: (b * qb_per_seq + i, h)),
        out_shape=jax.ShapeDtypeStruct((n_rows, BRANCH_WIDTH), BF16),
        compiler_params=_cparams(("parallel", "parallel", "parallel")),
        name="mla_attention",
    )(q, k, v)


def mla_mixer(pqkv, q_gain, kv_gain, w_uq, w_ukv, gq, gk, segs):
    q, k, v = mla_prep(pqkv, q_gain, kv_gain, w_uq, w_ukv, gq, gk, segs)
    outs = [attention_segment(q, k, v, f, nr, s) for (f, nr, s) in segs]
    return outs[0] if len(outs) == 1 else jnp.concatenate(outs, axis=0)


def _rms_rows(x, g):
    return x * lax.rsqrt(jnp.mean(x * x, axis=-1, keepdims=True) + EPS) * g


def _route_kernel(x_ref, g_ref, w_ref, b_ref, o_ref):
    xn = _rms_rows(x_ref[...], g_ref[...]).astype(BF16)
    lg = _dot(xn, w_ref[...]) + b_ref[...]
    lane = lax.broadcasted_iota(jnp.int32, lg.shape, 1)
    big = jnp.int32(1 << 20)
    is_g = (lane >= N_EXPERTS) & (lane < N_EXPERTS + N_GROUPS)
    gl = jnp.where(is_g, lg, NEG_INF)
    gmax = jnp.max(gl, axis=-1, keepdims=True)
    gidx = jnp.min(jnp.where(is_g & (gl == gmax), lane - N_EXPERTS, big), axis=-1, keepdims=True)
    gsum = jnp.sum(jnp.where(is_g, jnp.exp(gl - gmax), 0.0), axis=-1, keepdims=True)
    g_top = 1.0 / gsum
    in_grp = (lane < N_EXPERTS) & ((lane >> 3) == gidx)
    el = jnp.where(in_grp, lg, NEG_INF)
    v1 = jnp.max(el, axis=-1, keepdims=True)
    i1 = jnp.min(jnp.where(in_grp & (el == v1), lane, big), axis=-1, keepdims=True)
    rest = in_grp & (lane != i1)
    el2 = jnp.where(rest, lg, NEG_INF)
    v2 = jnp.max(el2, axis=-1, keepdims=True)
    i2 = jnp.min(jnp.where(rest & (el2 == v2), lane, big), axis=-1, keepdims=True)
    d = jnp.exp(v2 - v1)
    p1 = 1.0 / (1.0 + d)
    p2 = d / (1.0 + d)
    out = jnp.where(lane == 0, i1.astype(F32),
                    jnp.where(lane == 1, i2.astype(F32),
                              jnp.where(lane == 2, g_top * p1,
                                        jnp.where(lane == 3, g_top * p2, 0.0))))
    o_ref[...] = out


def moe_route(x, norm_g, w_group, b_group, w_router, b_router, tm=512):
    n = x.shape[0]
    tm = _tile(n, tm)
    padc = LANES - N_EXPERTS - N_GROUPS
    w = jnp.concatenate([w_router, w_group, jnp.zeros((D_MODEL, padc), F32)], axis=1).astype(BF16)
    b = jnp.concatenate([b_router, b_group, jnp.zeros((padc,), F32)]).reshape(1, LANES)
    r = pl.pallas_call(
        _route_kernel,
        grid=(n // tm,),
        in_specs=[pl.BlockSpec((tm, D_MODEL), lambda i: (i, 0)),
                  pl.BlockSpec((1, D_MODEL), lambda i: (0, 0)),
                  pl.BlockSpec((D_MODEL, LANES), lambda i: (0, 0)),
                  pl.BlockSpec((1, LANES), lambda i: (0, 0))],
        out_specs=pl.BlockSpec((tm, LANES), lambda i: (i, 0)),
        out_shape=jax.ShapeDtypeStruct((n, LANES), F32),
        compiler_params=_cparams(("parallel",)),
        name="moe_route",
    )(x, norm_g.reshape(1, D_MODEL), w, b)
    return r[:, :TOP_K].astype(jnp.int32), r[:, TOP_K:2 * TOP_K]


def _dispatch_plan(eid, gate, tm):
    n = eid.shape[0]
    m = n * TOP_K
    flat_e = eid.reshape(-1)
    onehot = (flat_e[:, None] == jnp.arange(N_EXPERTS, dtype=jnp.int32)[None, :]).astype(jnp.int32)
    csum = jnp.cumsum(onehot, axis=0)
    rank = jnp.take_along_axis(csum, flat_e[:, None], axis=1)[:, 0] - 1
    counts = csum[-1]
    padded = (counts + tm - 1) // tm * tm
    pad_end = jnp.cumsum(padded)
    pad_start = pad_end - padded
    dest = pad_start[flat_e] + rank
    p = (m + tm - 1) // tm * tm + N_EXPERTS * tm
    tok_buf = jnp.zeros((p,), jnp.int32).at[dest].set(jnp.arange(m, dtype=jnp.int32) // TOP_K)
    w_buf = jnp.zeros((p,), F32).at[dest].set(gate.reshape(-1))
    n_blocks = p // tm
    block_e = jnp.minimum(jnp.searchsorted(pad_end, jnp.arange(n_blocks, dtype=jnp.int32) * tm, side="right"),
                          N_EXPERTS - 1).astype(jnp.int32)
    n_used = (pad_end[-1] // tm).astype(jnp.int32).reshape(1)
    return tok_buf, w_buf, block_e, n_used, dest.reshape(n, TOP_K).astype(jnp.int32)


def _prefetched_rows(idx_ref, idx_next_ref, src_hbm, buf, sems, n_rows):
    i = pl.program_id(0)
    slot = lax.rem(i, 2)

    def row_copy(idx, r, s):
        return pltpu.make_async_copy(src_hbm.at[pl.ds(idx[0, 0, r], 1)], buf.at[s, pl.ds(r, 1)], sems.at[s])

    def start_block(idx, s):
        def body(r, carry):
            row_copy(idx, r, s).start()
            return carry
        lax.fori_loop(0, n_rows, body, 0)

    @pl.when(i == 0)
    def _():
        start_block(idx_ref, slot)

    @pl.when(i + 1 < pl.num_programs(0))
    def _():
        start_block(idx_next_ref, 1 - slot)

    def wait_row(r, carry):
        row_copy(idx_ref, r, slot).wait()
        return carry
    lax.fori_loop(0, n_rows, wait_row, 0)
    return slot


def _index_specs(n_rows, nb):
    cur = pl.BlockSpec((1, 1, n_rows), lambda i: (i, 0, 0), memory_space=pltpu.SMEM)
    nxt = pl.BlockSpec((1, 1, n_rows), lambda i: (jnp.minimum(i + 1, nb - 1), 0, 0), memory_space=pltpu.SMEM)
    return [cur, nxt]


def _gather_norm_kernel(tok_ref, tok_next_ref, x_hbm, g_ref, o_ref, buf, sems, *, rows):
    slot = _prefetched_rows(tok_ref, tok_next_ref, x_hbm, buf, sems, rows)
    o_ref[...] = _rms_rows(buf[slot], g_ref[...]).astype(o_ref.dtype)


def moe_gather_norm(x, norm_g, tok_buf, tm):
    p = tok_buf.shape[0]
    nb = p // tm
    tok = tok_buf.reshape(nb, 1, tm)
    return pl.pallas_call(
        functools.partial(_gather_norm_kernel, rows=tm),
        grid=(nb,),
        in_specs=_index_specs(tm, nb) + [pl.BlockSpec(memory_space=pl.ANY),
                                         pl.BlockSpec((1, D_MODEL), lambda i: (0, 0))],
        out_specs=pl.BlockSpec((tm, D_MODEL), lambda i: (i, 0)),
        scratch_shapes=[pltpu.VMEM((2, tm, D_MODEL), F32), pltpu.SemaphoreType.DMA((2,))],
        out_shape=jax.ShapeDtypeStruct((p, D_MODEL), BF16),
        compiler_params=_cparams(("arbitrary",)),
        name="moe_gather_norm",
    )(tok, tok, x, norm_g.reshape(1, D_MODEL))


def _moe_up_kernel(be_ref, nu_ref, xs_ref, wg_ref, wu_ref, h_ref, wg_bf, wu_bf):
    i = pl.program_id(1)

    @pl.when(i < nu_ref[0])
    def _():
        changed = (i == 0) | (be_ref[jnp.maximum(i - 1, 0)] != be_ref[i])

        @pl.when(changed)
        def _():
            wg_bf[...] = wg_ref[...].astype(BF16)
            wu_bf[...] = wu_ref[...].astype(BF16)

        x = xs_ref[...]
        g = _dot(x, wg_bf[...])
        u = _dot(x, wu_bf[...])
        h_ref[...] = (jax.nn.silu(g) * u).astype(h_ref.dtype)

    @pl.when(i >= nu_ref[0])
    def _():
        h_ref[...] = jnp.zeros_like(h_ref)


def moe_up(xs, w_gate, w_up, layer, block_e, n_used, tm, tn=512):
    p = xs.shape[0]
    nb = p // tm
    clamp = lambda i, nu: jnp.minimum(i, nu[0] - 1)
    w_spec = pl.BlockSpec((None, None, D_MODEL, tn), lambda j, i, be, nu: (layer, be[clamp(i, nu)], 0, j))
    return pl.pallas_call(
        _moe_up_kernel,
        grid_spec=pltpu.PrefetchScalarGridSpec(
            num_scalar_prefetch=2,
            grid=(EXPERT_FF // tn, nb),
            in_specs=[pl.BlockSpec((tm, D_MODEL), lambda j, i, be, nu: (clamp(i, nu), 0)), w_spec, w_spec],
            out_specs=pl.BlockSpec((tm, tn), lambda j, i, be, nu: (i, j)),
            scratch_shapes=[pltpu.VMEM((D_MODEL, tn), BF16), pltpu.VMEM((D_MODEL, tn), BF16)]),
        out_shape=jax.ShapeDtypeStruct((p, EXPERT_FF), BF16),
        compiler_params=_cparams(("arbitrary", "arbitrary")),
        name="moe_up",
    )(block_e, n_used, xs, w_gate, w_up)


def _moe_down_kernel(be_ref, nu_ref, h_ref, wd_ref, w_ref, y_ref, wd_bf):
    i = pl.program_id(1)

    @pl.when(i < nu_ref[0])
    def _():
        changed = (i == 0) | (be_ref[jnp.maximum(i - 1, 0)] != be_ref[i])

        @pl.when(changed)
        def _():
            wd_bf[...] = wd_ref[...].astype(BF16)

        y_ref[...] = _dot(h_ref[...], wd_bf[...]) * w_ref[...]

    @pl.when(i >= nu_ref[0])
    def _():
        y_ref[...] = jnp.zeros_like(y_ref)


def moe_down(h, w_down, layer, w_buf, block_e, n_used, tm, tn=2048):
    p = h.shape[0]
    nb = p // tm
    clamp = lambda i, nu: jnp.minimum(i, nu[0] - 1)
    return pl.pallas_call(
        _moe_down_kernel,
        grid_spec=pltpu.PrefetchScalarGridSpec(
            num_scalar_prefetch=2,
            grid=(D_MODEL // tn, nb),
            in_specs=[pl.BlockSpec((tm, EXPERT_FF), lambda j, i, be, nu: (clamp(i, nu), 0)),
                      pl.BlockSpec((None, None, EXPERT_FF, tn),
                                   lambda j, i, be, nu: (layer, be[clamp(i, nu)], 0, j)),
                      pl.BlockSpec((tm, 1), lambda j, i, be, nu: (clamp(i, nu), 0))],
            out_specs=pl.BlockSpec((tm, tn), lambda j, i, be, nu: (i, j)),
            scratch_shapes=[pltpu.VMEM((EXPERT_FF, tn), BF16)]),
        out_shape=jax.ShapeDtypeStruct((p, D_MODEL), F32),
        compiler_params=_cparams(("arbitrary", "arbitrary")),
        name="moe_down",
    )(block_e, n_used, h, w_down, w_buf.reshape(p, 1))


def _combine_kernel(pos_ref, pos_next_ref, x_ref, ys_hbm, o_ref, buf, sems, *, tc):
    slot = _prefetched_rows(pos_ref, pos_next_ref, ys_hbm, buf, sems, TOP_K * tc)
    o_ref[...] = x_ref[...] + buf[slot, pl.ds(0, tc)] + buf[slot, pl.ds(tc, tc)]


def moe_combine(x, ys, pos, tc=128):
    n = x.shape[0]
    tc = _tile(n, tc)
    nb = n // tc
    pos_blk = jnp.swapaxes(pos.reshape(nb, tc, TOP_K), 1, 2).reshape(nb, 1, TOP_K * tc)
    blk = pl.BlockSpec((tc, D_MODEL), lambda i: (i, 0))
    return pl.pallas_call(
        functools.partial(_combine_kernel, tc=tc),
        grid=(nb,),
        in_specs=_index_specs(TOP_K * tc, nb) + [blk, pl.BlockSpec(memory_space=pl.ANY)],
        out_specs=blk,
        out_shape=jax.ShapeDtypeStruct((n, D_MODEL), F32),
        scratch_shapes=[pltpu.VMEM((2, TOP_K * tc, D_MODEL), F32), pltpu.SemaphoreType.DMA((2,))],
        compiler_params=_cparams(("arbitrary",)),
        name="moe_combine",
    )(pos_blk, pos_blk, x, ys)


def hier_moe_residual(x, norm_g, layer, w_group, b_group, w_router, b_router, w_gate, w_up, w_down,
                      tm=MOE_BLOCK):
    eid, gate = moe_route(x, norm_g, w_group, b_group, w_router, b_router)
    tok_buf, w_buf, block_e, n_used, pos = _dispatch_plan(eid, gate, tm)
    xs = moe_gather_norm(x, norm_g, tok_buf, tm)
    h = moe_up(xs, w_gate, w_up, layer, block_e, n_used, tm)
    ys = moe_down(h, w_down, layer, w_buf, block_e, n_used, tm)
    return moe_combine(x, ys, pos)


def encoder_layer(x, layer, segs, lb_f, lb_b, norm_mix, w_in, gmlp_v_norm, gmlp_w_s, gmlp_b_s,
                  hgrn_out_norm, conv_w, mla_q_norm, mla_kv_norm, mla_w_uq, mla_w_ukv, qk_norm_q,
                  qk_norm_k, w_branch, w_out, norm_ffn, w_group, b_group, w_router, b_router,
                  w_gate, w_up, w_down):
    w_in_l = w_in[layer]
    w_abc = w_in_l[:, :ABC_WIDTH].astype(BF16)
    w_qkv = jnp.pad(w_in_l[:, ABC_WIDTH:ABC_WIDTH + QKV_WIDTH], ((0, 0), (0, QKV_PAD - QKV_WIDTH))).astype(BF16)
    w_gl = w_in_l[:, ABC_WIDTH + QKV_WIDTH:].astype(BF16)

    xn = rmsnorm_rows(x, norm_mix[layer])
    pabc = matmul(xn, w_abc, out_dtype=F32, tm=1024, tn=512, name="proj_abc")
    pqkv = matmul(xn, w_qkv, out_dtype=F32, tm=1024, tn=384, name="proj_qkv")
    gates = matmul(xn, w_gl, out_dtype=BF16, tm=1024, tn=512, act="sigmoid", name="proj_gates")

    branches = (
        gmlp_mixer(pabc, gmlp_v_norm[layer], gmlp_w_s[layer], gmlp_b_s[layer]),
        hgrn_mixer(pabc, lb_f, lb_b, hgrn_out_norm[layer], segs),
        conv_mixer(pabc, conv_w[layer], segs),
        mla_mixer(pqkv, mla_q_norm[layer], mla_kv_norm[layer], mla_w_uq[layer], mla_w_ukv[layer],
                  qk_norm_q[layer], qk_norm_k[layer], segs),
    )
    merged = merge_branches(branches, w_branch[layer].astype(BF16), gates)
    x = matmul(merged, w_out[layer].astype(BF16), out_dtype=F32, tm=1024, tn=512, res=x, name="out_proj")
    return hier_moe_residual(x, norm_ffn[layer], layer, w_group[layer], b_group[layer], w_router[layer],
                             b_router[layer], w_gate, w_up, w_down)


def kernel(x_prompt, x_sample, norm_mix, w_in, gmlp_v_norm, gmlp_w_s, gmlp_b_s, hgrn_lb_logits, hgrn_out_norm, conv_w, mla_q_norm, mla_kv_norm, mla_w_uq, mla_w_ukv, qk_norm_q, qk_norm_k, w_branch, w_out, norm_ffn, w_group, b_group, w_router, b_router, w_gate, w_up, w_down):
    depth = norm_mix.shape[0]
    bp, sp, d = x_prompt.shape
    bs, ss, _ = x_sample.shape
    n_p, n_s = bp * sp, bs * ss
    segs = ((0, n_p, sp), (n_p, n_s, ss))
    x = jnp.concatenate([x_prompt.reshape(n_p, d), x_sample.reshape(n_s, d)], axis=0)

    lb_p = jax.nn.softmax(hgrn_lb_logits.astype(F32), axis=1)
    lb = jnp.maximum(jnp.cumsum(lb_p, axis=1) - lb_p[:, :1], 0.0)

    for layer in range(depth):
        x = encoder_layer(x, layer, segs, lb[0, layer], lb[1, layer], norm_mix, w_in, gmlp_v_norm, gmlp_w_s,
                          gmlp_b_s, hgrn_out_norm, conv_w, mla_q_norm, mla_kv_norm, mla_w_uq, mla_w_ukv,
                          qk_norm_q, qk_norm_k, w_branch, w_out, norm_ffn, w_group, b_group, w_router,
                          b_router, w_gate, w_up, w_down)
    return (x[:n_p].reshape(bp, sp, d), x[n_p:].reshape(bs, ss, d))
```

```python
import functools
import math

import jax
import jax.numpy as jnp
import numpy as np
from jax import lax
from jax.experimental import pallas as pl
from jax.experimental.pallas import tpu as pltpu

F32 = jnp.float32
BF16 = jnp.bfloat16

D_MODEL = 4096
BRANCH_WIDTH = 1024
HEAD_DIM = 128
N_HEADS = 8
GMLP_CHUNK = 128
MLA_Q_RANK = 768
MLA_KV_RANK = 256
MLA_NOPE = 128
MLA_ROPE = 64
MLA_QK = MLA_NOPE + MLA_ROPE
MLA_QK_PAD = 256
ROPE_BASE = 10000.0
N_GROUPS = 8
EXPERTS_PER_GROUP = 8
N_EXPERTS = 64
TOP_K = 2
EXPERT_FF = 1024
EPS = 1e-6
ABC_WIDTH = 10 * BRANCH_WIDTH
QKV_WIDTH = MLA_Q_RANK + MLA_KV_RANK + MLA_ROPE
QKV_PAD = 1152
LANES = 128
VMEM_LIMIT = 56 * 1024 * 1024
HGRN_CHUNK = 64
MOE_BLOCK = 256
NEG_INF = float("-inf")


def _cparams(sem):
    return pltpu.CompilerParams(dimension_semantics=sem, vmem_limit_bytes=VMEM_LIMIT)


def _tile(n, t):
    t = min(t, n)
    while n % t:
        t //= 2
    return t


def _dot(a, b):
    return jnp.dot(a, b, preferred_element_type=F32)


def _dot_nt(a, b):
    return lax.dot_general(a, b, (((1,), (1,)), ((), ())), preferred_element_type=F32)


def _rmsnorm_kernel(x_ref, g_ref, o_ref):
    x = x_ref[...]
    ms = jnp.mean(x * x, axis=-1, keepdims=True)
    o_ref[...] = (x * lax.rsqrt(ms + EPS) * g_ref[...]).astype(o_ref.dtype)


def rmsnorm_rows(x, g, tm=512):
    n, d = x.shape
    tm = _tile(n, tm)
    return pl.pallas_call(
        _rmsnorm_kernel,
        grid=(n // tm,),
        in_specs=[pl.BlockSpec((tm, d), lambda i: (i, 0)),
                  pl.BlockSpec((1, d), lambda i: (0, 0))],
        out_specs=pl.BlockSpec((tm, d), lambda i: (i, 0)),
        out_shape=jax.ShapeDtypeStruct((n, d), BF16),
        compiler_params=_cparams(("parallel",)),
        name="rmsnorm",
    )(x, g.reshape(1, d))


def _mm_kernel(x_ref, w_ref, *rest, act, has_res):
    o_ref = rest[-1]
    acc = _dot(x_ref[...], w_ref[...])
    if act == "sigmoid":
        acc = jax.nn.sigmoid(acc)
    if has_res:
        acc = acc + rest[0][...]
    o_ref[...] = acc.astype(o_ref.dtype)


def matmul(x, w, *, out_dtype, tm, tn, act=None, res=None, name="matmul"):
    m, k = x.shape
    n = w.shape[1]
    tm = _tile(m, tm)
    in_specs = [pl.BlockSpec((tm, k), lambda i, j: (i, 0)),
                pl.BlockSpec((k, tn), lambda i, j: (0, j))]
    args = [x, w]
    if res is not None:
        in_specs.append(pl.BlockSpec((tm, tn), lambda i, j: (i, j)))
        args.append(res)
    return pl.pallas_call(
        functools.partial(_mm_kernel, act=act, has_res=res is not None),
        grid=(m // tm, n // tn),
        in_specs=in_specs,
        out_specs=pl.BlockSpec((tm, tn), lambda i, j: (i, j)),
        out_shape=jax.ShapeDtypeStruct((m, n), out_dtype),
        compiler_params=_cparams(("parallel", "parallel")),
        name=name,
    )(*args)


def _merge_kernel(b0, b1, b2, b3, wb_ref, g0, g1, g2, g3, o_ref):
    acc = None
    for kb, (b_ref, g_ref) in enumerate(((b0, g0), (b1, g1), (b2, g2), (b3, g3))):
        t = _dot(b_ref[...], wb_ref[kb]) * g_ref[...].astype(F32)
        acc = t if acc is None else acc + t
    o_ref[...] = acc.astype(o_ref.dtype)


def merge_branches(branches, w_branch, gates, tm=1024, tn=512):
    n = branches[0].shape[0]
    tm = _tile(n, tm)
    nj = D_MODEL // tn
    b_spec = pl.BlockSpec((tm, BRANCH_WIDTH), lambda i, j: (i, 0))
    g_specs = [pl.BlockSpec((tm, tn), functools.partial(lambda i, j, kb: (i, kb * nj + j), kb=kb))
               for kb in range(4)]
    return pl.pallas_call(
        _merge_kernel,
        grid=(n // tm, nj),
        in_specs=[b_spec] * 4 + [pl.BlockSpec((4, BRANCH_WIDTH, tn), lambda i, j: (0, 0, j))] + g_specs,
        out_specs=pl.BlockSpec((tm, tn), lambda i, j: (i, j)),
        out_shape=jax.ShapeDtypeStruct((n, D_MODEL), BF16),
        compiler_params=_cparams(("parallel", "parallel")),
        name="merge_branches",
    )(*branches, w_branch, gates, gates, gates, gates)


def _gmlp_kernel(u_ref, v_ref, g_ref, ws_ref, bs_ref, o_ref, *, tm):
    for c in range(tm // GMLP_CHUNK):
        rows = pl.ds(c * GMLP_CHUNK, GMLP_CHUNK)
        u = jax.nn.gelu(u_ref[rows, :])
        v = jax.nn.gelu(v_ref[rows, :])
        ms = jnp.mean(v * v, axis=-1, keepdims=True)
        vb = (v * lax.rsqrt(ms + EPS) * g_ref[...]).astype(BF16)
        for h in range(N_HEADS):
            cols = slice(h * HEAD_DIM, (h + 1) * HEAD_DIM)
            mixed = _dot(ws_ref[h], vb[:, cols]) + bs_ref[:, cols]
            o_ref[rows, cols] = (u[:, cols] * mixed).astype(o_ref.dtype)


def gmlp_mixer(pabc, v_gain, w_s, b_s, tm=512):
    n = pabc.shape[0]
    tm = _tile(n, tm)
    bias = jnp.repeat(jnp.swapaxes(b_s, 0, 1), HEAD_DIM, axis=1)
    return pl.pallas_call(
        functools.partial(_gmlp_kernel, tm=tm),
        grid=(n // tm,),
        in_specs=[pl.BlockSpec((tm, BRANCH_WIDTH), lambda i: (i, 0)),
                  pl.BlockSpec((tm, BRANCH_WIDTH), lambda i: (i, 1)),
                  pl.BlockSpec((1, BRANCH_WIDTH), lambda i: (0, 0)),
                  pl.BlockSpec((N_HEADS, GMLP_CHUNK, GMLP_CHUNK), lambda i: (0, 0, 0)),
                  pl.BlockSpec((GMLP_CHUNK, BRANCH_WIDTH), lambda i: (0, 0))],
        out_specs=pl.BlockSpec((tm, BRANCH_WIDTH), lambda i: (i, 0)),
        out_shape=jax.ShapeDtypeStruct((n, BRANCH_WIDTH), BF16),
        compiler_params=_cparams(("parallel",)),
        name="gmlp_mixer",
    )(pabc, pabc, v_gain.reshape(1, -1), w_s.astype(BF16), bias)


def _segment_of(row0, segs):
    seq_len = jnp.int32(segs[-1][2])
    first = jnp.int32(segs[-1][0])
    for (f, nr, s) in reversed(segs[:-1]):
        inside = row0 < f + nr
        seq_len = jnp.where(inside, s, seq_len)
        first = jnp.where(inside, f, first)
    return seq_len, first


def _conv_kernel(bg_ref, cg_ref, h_ref, cgp_ref, hp_ref, cgn_ref, hn_ref, w_ref, o_ref, *, tm, segs):
    row0 = pl.program_id(0) * tm
    seq_len, first = _segment_of(row0, segs)
    at_start = lax.rem(row0 - first, seq_len) == 0
    at_end = lax.rem(row0 - first + tm, seq_len) == 0
    z = cg_ref[...] * h_ref[...]
    left = jnp.where(at_start, 0.0, cgp_ref[7:8, :] * hp_ref[7:8, :])
    right = jnp.where(at_end, 0.0, cgn_ref[0:1, :] * hn_ref[0:1, :])
    ridx = lax.broadcasted_iota(jnp.int32, z.shape, 0)
    z_prev = jnp.where(ridx == 0, left, pltpu.roll(z, 1, axis=0))
    z_next = jnp.where(ridx == tm - 1, right, pltpu.roll(z, tm - 1, axis=0))
    y = w_ref[0:1, :] * z_prev + w_ref[1:2, :] * z + w_ref[2:3, :] * z_next
    o_ref[...] = (bg_ref[...] * y).astype(o_ref.dtype)


def conv_mixer(pabc, conv_w, segs, tm=256):
    n = pabc.shape[0]
    tm = _tile(n, tm)
    nb8 = n // 8
    blk = lambda c: pl.BlockSpec((tm, BRANCH_WIDTH), lambda i: (i, c))
    prev8 = lambda c: pl.BlockSpec((8, BRANCH_WIDTH), lambda i: (jnp.maximum(i * (tm // 8) - 1, 0), c))
    next8 = lambda c: pl.BlockSpec((8, BRANCH_WIDTH), lambda i: (jnp.minimum((i + 1) * (tm // 8), nb8 - 1), c))
    return pl.pallas_call(
        functools.partial(_conv_kernel, tm=tm, segs=segs),
        grid=(n // tm,),
        in_specs=[blk(7), blk(8), blk(9), prev8(8), prev8(9), next8(8), next8(9),
                  pl.BlockSpec((3, BRANCH_WIDTH), lambda i: (0, 0))],
        out_specs=pl.BlockSpec((tm, BRANCH_WIDTH), lambda i: (i, 0)),
        out_shape=jax.ShapeDtypeStruct((n, BRANCH_WIDTH), BF16),
        compiler_params=_cparams(("parallel",)),
        name="conv_mixer",
    )(pabc, pabc, pabc, pabc, pabc, pabc, pabc, conv_w)


def _hgrn_constants(c, reverse):
    n_lev = int(math.log2(c))
    t = np.arange(c)[:, None]
    u = np.arange(c)[None, :]
    if not reverse:
        w_a = u <= t
        w_kd = u > t
    else:
        w_a = u >= t
        w_kd = u < t
    ws = [w_a, w_kd]
    masks = [np.eye(c, dtype=bool)]
    for lev in range(n_lev):
        m = c >> (lev + 1)
        blk_t = t // (2 * m)
        upper_t = (t % (2 * m)) >= m
        p_last_lower = blk_t * 2 * m + m - 1
        if not reverse:
            w_q = (u > p_last_lower) & (u <= t)
            w_k = (u > t) & (u <= p_last_lower)
            w = np.where(upper_t, w_q, w_k)
            query_t = upper_t
        else:
            p_first_upper = p_last_lower + 1
            w_q = (u >= t) & (u < p_first_upper)
            w_k = (u >= p_first_upper) & (u < t)
            w = np.where(upper_t, w_k, w_q)
            query_t = ~upper_t
        ws.append(w)
        same_blk = blk_t == blk_t.T
        masks.append(same_blk & query_t & (~query_t).T)
    return np.stack(ws).astype(np.float32), np.stack(masks).astype(np.float32)


def _hgrn_pair_constants(c):
    ws_f, masks_f = _hgrn_constants(c, False)
    ws_b, masks_b = _hgrn_constants(c, True)

    def blockdiag(a, b):
        out = np.zeros((a.shape[0], 2 * c, 2 * c), np.float32)
        out[:, :c, :c] = a
        out[:, c:, c:] = b
        return out

    w_all = blockdiag(ws_f, ws_b).reshape(-1, 2 * c)
    return jnp.asarray(w_all, BF16), jnp.asarray(blockdiag(masks_f, masks_b), F32)


def _log_forget_and_key(z, log_lb, log1m_lb):
    log_sig = jnp.minimum(z, 0.0) - jnp.log1p(jnp.exp(-jnp.abs(z)))
    x2 = log1m_lb + log_sig
    hi = jnp.maximum(log_lb, x2)
    lf = hi + jnp.log1p(jnp.exp(-jnp.abs(log_lb - x2)))
    return lf, jnp.exp(x2 - z)


def _split3(x):
    p1 = x.astype(BF16)
    r1 = x - p1.astype(F32)
    p2 = r1.astype(BF16)
    p3 = (r1 - p2.astype(F32)).astype(BF16)
    return p1, p2, p3


def _hgrn_kernel(qf_ref, vf_ref, zf_ref, qb_ref, vb_ref, zb_ref, lb_ref, wall_ref, mask_ref,
                 of_ref, ob_ref, st_ref, *, c, n_lev, segs):
    step = pl.program_id(0)
    row_f = step * c
    row_b = (pl.num_programs(0) - 1 - step) * c
    len_f, first_f = _segment_of(row_f, segs)
    len_b, first_b = _segment_of(row_b, segs)

    @pl.when(lax.rem(row_f - first_f, len_f) == 0)
    def _():
        st_ref[0] = jnp.zeros(st_ref.shape[1:], F32)

    @pl.when(lax.rem(row_b - first_b + c, len_b) == 0)
    def _():
        st_ref[1] = jnp.zeros(st_ref.shape[1:], F32)

    w_all = wall_ref[...]
    q = jnp.concatenate([qf_ref[...], qb_ref[...]], axis=0)
    v = jnp.concatenate([vf_ref[...], vb_ref[...]], axis=0)
    z = jnp.concatenate([zf_ref[...], zb_ref[...]], axis=0)
    row = lax.broadcasted_iota(jnp.int32, q.shape, 0)
    is_fwd = row < c
    lf, k = _log_forget_and_key(z, jnp.where(is_fwd, lb_ref[0:1, :], lb_ref[2:3, :]),
                                jnp.where(is_fwd, lb_ref[1:2, :], lb_ref[3:4, :]))
    p1, p2, p3 = _split3(lf)
    e_all = _dot(w_all, p1) + _dot(w_all, p2) + _dot(w_all, p3)
    c2 = 2 * c
    a = e_all[0:c2]
    qd = (q * jnp.exp(a)).astype(BF16)
    kd = (k * jnp.exp(e_all[c2:2 * c2])).astype(BF16)
    q16 = q.astype(BF16)
    k16 = k.astype(BF16)
    v16 = v.astype(BF16)
    xms = []
    for lev in range(n_lev):
        upper = ((row >> (n_lev - 1 - lev)) & 1) == 1
        is_query = upper == is_fwd
        dm = jnp.exp(e_all[(2 + lev) * c2:(3 + lev) * c2])
        xms.append((jnp.where(is_query, q, k) * dm).astype(BF16))
    decay_f = jnp.exp(a[c - 1:c])
    decay_b = jnp.exp(a[c:c + 1])
    for h in range(N_HEADS):
        cols = slice(h * HEAD_DIM, (h + 1) * HEAD_DIM)
        scores = mask_ref[0] * _dot_nt(q16[:, cols], k16[:, cols])
        for lev in range(n_lev):
            xm = xms[lev][:, cols]
            scores = scores + mask_ref[lev + 1] * _dot_nt(xm, xm)
        o_intra = _dot(scores.astype(BF16), v16[:, cols])
        st_f = st_ref[0, h]
        st_b = st_ref[1, h]
        of_ref[:, cols] = o_intra[:c] + _dot_nt(qd[:c, cols], st_f.astype(BF16))
        ob_ref[:, cols] = o_intra[c:] + _dot_nt(qd[c:, cols], st_b.astype(BF16))
        st_ref[0, h] = st_f * decay_f[:, cols] + _dot(v[:c, cols].T.astype(BF16), kd[:c, cols])
        st_ref[1, h] = st_b * decay_b[:, cols] + _dot(v[c:, cols].T.astype(BF16), kd[c:, cols])


def _hgrn_out_kernel(of_ref, ob_ref, g_ref, gain_ref, o_ref):
    for h in range(N_HEADS):
        cols = slice(h * HEAD_DIM, (h + 1) * HEAD_DIM)
        o = of_ref[:, cols] + ob_ref[:, cols]
        ms = jnp.mean(o * o, axis=-1, keepdims=True)
        o_ref[:, cols] = (o * lax.rsqrt(ms + EPS) * gain_ref[...] * jax.nn.silu(g_ref[:, cols])).astype(o_ref.dtype)


def hgrn_mixer(pabc, lb_f, lb_b, out_gain, segs, c=HGRN_CHUNK, tm=512):
    n = pabc.shape[0]
    n_lev = int(math.log2(c))
    n_chunks = n // c
    w_all, masks = _hgrn_pair_constants(c)
    log_lb = jnp.stack([jnp.log(lb_f), jnp.log1p(-lb_f), jnp.log(lb_b), jnp.log1p(-lb_b)])
    fwd = lambda col: pl.BlockSpec((c, BRANCH_WIDTH), lambda i: (i, col))
    bwd = lambda col: pl.BlockSpec((c, BRANCH_WIDTH), lambda i: (n_chunks - 1 - i, col))
    o_f, o_b = pl.pallas_call(
        functools.partial(_hgrn_kernel, c=c, n_lev=n_lev, segs=segs),
        grid=(n_chunks,),
        in_specs=[fwd(2), fwd(3), fwd(4), bwd(2), bwd(3), bwd(5),
                  pl.BlockSpec((4, BRANCH_WIDTH), lambda i: (0, 0)),
                  pl.BlockSpec(w_all.shape, lambda i: (0, 0)),
                  pl.BlockSpec(masks.shape, lambda i: (0, 0, 0))],
        out_specs=[fwd(0), bwd(0)],
        out_shape=[jax.ShapeDtypeStruct((n, BRANCH_WIDTH), F32)] * 2,
        scratch_shapes=[pltpu.VMEM((2, N_HEADS, HEAD_DIM, HEAD_DIM), F32)],
        compiler_params=_cparams(("arbitrary",)),
        name="hgrn_scan",
    )(pabc, pabc, pabc, pabc, pabc, pabc, log_lb, w_all, masks)
    tm = _tile(n, tm)
    blk = lambda col: pl.BlockSpec((tm, BRANCH_WIDTH), lambda i: (i, col))
    return pl.pallas_call(
        _hgrn_out_kernel,
        grid=(n // tm,),
        in_specs=[blk(0), blk(0), blk(6), pl.BlockSpec((1, HEAD_DIM), lambda i: (0, 0))],
        out_specs=blk(0),
        out_shape=jax.ShapeDtypeStruct((n, BRANCH_WIDTH), BF16),
        compiler_params=_cparams(("parallel",)),
        name="hgrn_out",
    )(o_f, o_b, pabc, out_gain.reshape(1, HEAD_DIM))


def _rope_tile(r, cos_t, sin_a, sin_b):
    return r * cos_t + pltpu.roll(r, 96, axis=1) * sin_a + pltpu.roll(r, 32, axis=1) * sin_b


def _mla_prep_kernel(cq_ref, ckv_ref, wq_ref, wkv_ref, qg_ref, kvg_ref, gq_ref, gk_ref,
                     cos_ref, sina_ref, sinb_ref, q_out, k_out, v_out):
    cq = cq_ref[...]
    hq = (cq * lax.rsqrt(jnp.mean(cq * cq, axis=-1, keepdims=True) + EPS) * qg_ref[...]).astype(BF16)
    qf = _dot(hq, wq_ref[...])
    ckvkr = ckv_ref[...]
    ckv = ckvkr[:, :MLA_KV_RANK]
    kr = ckvkr[:, MLA_KV_RANK:]
    hk = (ckv * lax.rsqrt(jnp.mean(ckv * ckv, axis=-1, keepdims=True) + EPS) * kvg_ref[...]).astype(BF16)
    kvf = _dot(hk, wkv_ref[...])
    cos_t, sin_a, sin_b = cos_ref[...], sina_ref[...], sinb_ref[...]
    gq_n, gq_r = gq_ref[:, :MLA_NOPE], gq_ref[:, MLA_NOPE:]
    gk_n, gk_r = gk_ref[:, :MLA_NOPE], gk_ref[:, MLA_NOPE:]
    kr_ss = jnp.sum(kr * kr, axis=-1, keepdims=True)
    kr_rot = _rope_tile(kr * gk_r, cos_t, sin_a, sin_b)
    scale = MLA_QK ** -0.5 * math.log2(math.e)
    lane = lax.broadcasted_iota(jnp.int32, (cq.shape[0], HEAD_DIM), 1)
    ones_col = jnp.where(lane == 0, 1.0, 0.0).astype(BF16)
    for h in range(N_HEADS):
        base = h * MLA_QK_PAD
        qn = qf[:, base:base + MLA_NOPE]
        qr = qf[:, base + MLA_NOPE:base + MLA_QK_PAD]
        ss = jnp.sum(qn * qn, axis=-1, keepdims=True) + jnp.sum(qr * qr, axis=-1, keepdims=True)
        rstd = lax.rsqrt(ss * (1.0 / MLA_QK) + EPS) * scale
        q_out[h, :, :MLA_NOPE] = (qn * rstd * gq_n).astype(BF16)
        q_out[h, :, MLA_NOPE:] = _rope_tile(qr * rstd * gq_r, cos_t, sin_a, sin_b).astype(BF16)
        kn = kvf[:, base:base + MLA_NOPE]
        ssk = jnp.sum(kn * kn, axis=-1, keepdims=True) + kr_ss
        rstdk = lax.rsqrt(ssk * (1.0 / MLA_QK) + EPS)
        k_out[h, :, :MLA_NOPE] = (kn * rstdk * gk_n).astype(BF16)
        k_out[h, :, MLA_NOPE:] = (kr_rot * rstdk).astype(BF16)
        v_out[h, :, :HEAD_DIM] = kvf[:, base + MLA_NOPE:base + MLA_QK_PAD].astype(BF16)
        v_out[h, :, HEAD_DIM:] = ones_col


def _rope_tables(segs):
    half = MLA_ROPE // 2
    inv_freq = ROPE_BASE ** (-jnp.arange(half, dtype=F32) * 2.0 / MLA_ROPE)
    pos = jnp.concatenate([jnp.tile(jnp.arange(s, dtype=F32), nr // s) for (_, nr, s) in segs])
    ang = pos[:, None] * inv_freq[None, :]
    cos, sin = jnp.cos(ang), jnp.sin(ang)
    zero = jnp.zeros_like(cos)
    zero2 = jnp.concatenate([zero, zero], axis=1)
    cos_t = jnp.concatenate([cos, cos, zero2], axis=1)
    sin_a = jnp.concatenate([-sin, zero, zero2], axis=1)
    sin_b = jnp.concatenate([zero, sin, zero2], axis=1)
    return cos_t, sin_a, sin_b


def mla_prep(pqkv, q_gain, kv_gain, w_uq, w_ukv, gq, gk, segs, tm=256):
    n = pqkv.shape[0]
    tm = _tile(n, tm)
    wq = jnp.pad(w_uq.reshape(MLA_Q_RANK, N_HEADS, MLA_QK),
                 ((0, 0), (0, 0), (0, MLA_QK_PAD - MLA_QK))).reshape(MLA_Q_RANK, -1).astype(BF16)
    wkv = w_ukv.astype(BF16)
    pad = lambda g: jnp.pad(g, (0, MLA_QK_PAD - MLA_QK)).reshape(1, MLA_QK_PAD)
    cos_t, sin_a, sin_b = _rope_tables(segs)
    const = lambda shape: pl.BlockSpec(shape, lambda i: (0, 0))
    tab = pl.BlockSpec((tm, LANES), lambda i: (i, 0))
    hd = lambda w: pl.BlockSpec((N_HEADS, tm, w), lambda i: (0, i, 0))
    return pl.pallas_call(
        _mla_prep_kernel,
        grid=(n // tm,),
        in_specs=[pl.BlockSpec((tm, MLA_Q_RANK), lambda i: (i, 0)),
                  pl.BlockSpec((tm, QKV_PAD - MLA_Q_RANK), lambda i: (i, 2)),
                  const(wq.shape), const(wkv.shape),
                  const((1, MLA_Q_RANK)), const((1, MLA_KV_RANK)),
                  const((1, MLA_QK_PAD)), const((1, MLA_QK_PAD)), tab, tab, tab],
        out_specs=[hd(MLA_QK_PAD), hd(MLA_QK_PAD), hd(2 * HEAD_DIM)],
        out_shape=[jax.ShapeDtypeStruct((N_HEADS, n, MLA_QK_PAD), BF16),
                   jax.ShapeDtypeStruct((N_HEADS, n, MLA_QK_PAD), BF16),
                   jax.ShapeDtypeStruct((N_HEADS, n, 2 * HEAD_DIM), BF16)],
        compiler_params=_cparams(("parallel",)),
        name="mla_prep",
    )(pqkv, pqkv, wq, wkv, q_gain.reshape(1, -1), kv_gain.reshape(1, -1), pad(gq), pad(gk),
      cos_t, sin_a, sin_b)


def _attn_kernel(q_ref, k_ref, v_ref, o_ref, *, kv_chunk):
    q = q_ref[...]
    acc = None
    m_run = None
    for c0 in range(0, k_ref.shape[0], kv_chunk):
        s = _dot_nt(q, k_ref[pl.ds(c0, kv_chunk), :])
        m = jnp.max(s, axis=-1, keepdims=True)
        ol = _dot(jnp.exp2(s - m).astype(BF16), v_ref[pl.ds(c0, kv_chunk), :])
        if acc is None:
            acc, m_run = ol, m
        else:
            m_new = jnp.maximum(m_run, m)
            acc = acc * jnp.exp2(m_run - m_new) + ol * jnp.exp2(m - m_new)
            m_run = m_new
    o_ref[...] = (acc[:, :HEAD_DIM] / acc[:, HEAD_DIM:HEAD_DIM + 1]).astype(o_ref.dtype)


def attention_segment(q, k, v, first_row, n_rows, seq_len, tq=512, kv_chunk=1024):
    tq = min(tq, seq_len)
    kv_chunk = min(kv_chunk, seq_len)
    n_seq = n_rows // seq_len
    qb_per_seq = seq_len // tq
    qoff = first_row // tq
    soff = first_row // seq_len
    return pl.pallas_call(
        functools.partial(_attn_kernel, kv_chunk=kv_chunk),
        grid=(n_seq, N_HEADS, qb_per_seq),
        in_specs=[pl.BlockSpec((None, tq, MLA_QK_PAD), lambda b, h, i: (h, qoff + b * qb_per_seq + i, 0)),
                  pl.BlockSpec((None, seq_len, MLA_QK_PAD), lambda b, h, i: (h, soff + b, 0)),
                  pl.BlockSpec((None, seq_len, 2 * HEAD_DIM), lambda b, h, i: (h, soff + b, 0))],
        out_specs=pl.BlockSpec((tq, HEAD_DIM), lambda b, h, i: (b * qb_per_seq + i, h)),
        out_shape=jax.ShapeDtypeStruct((n_rows, BRANCH_WIDTH), BF16),
        compiler_params=_cparams(("parallel", "parallel", "parallel")),
        name="mla_attention",
    )(q, k, v)


def mla_mixer(pqkv, q_gain, kv_gain, w_uq, w_ukv, gq, gk, segs):
    q, k, v = mla_prep(pqkv, q_gain, kv_gain, w_uq, w_ukv, gq, gk, segs)
    outs = [attention_segment(q, k, v, f, nr, s) for (f, nr, s) in segs]
    return outs[0] if len(outs) == 1 else jnp.concatenate(outs, axis=0)


def _rms_rows(x, g):
    return x * lax.rsqrt(jnp.mean(x * x, axis=-1, keepdims=True) + EPS) * g


def _route_kernel(x_ref, g_ref, w_ref, b_ref, o_ref):
    xn = _rms_rows(x_ref[...], g_ref[...]).astype(BF16)
    lg = _dot(xn, w_ref[...]) + b_ref[...]
    lane = lax.broadcasted_iota(jnp.int32, lg.shape, 1)
    big = jnp.int32(1 << 20)
    is_g = (lane >= N_EXPERTS) & (lane < N_EXPERTS + N_GROUPS)
    gl = jnp.where(is_g, lg, NEG_INF)
    gmax = jnp.max(gl, axis=-1, keepdims=True)
    gidx = jnp.min(jnp.where(is_g & (gl == gmax), lane - N_EXPERTS, big), axis=-1, keepdims=True)
    gsum = jnp.sum(jnp.where(is_g, jnp.exp(gl - gmax), 0.0), axis=-1, keepdims=True)
    g_top = 1.0 / gsum
    in_grp = (lane < N_EXPERTS) & ((lane >> 3) == gidx)
    el = jnp.where(in_grp, lg, NEG_INF)
    v1 = jnp.max(el, axis=-1, keepdims=True)
    i1 = jnp.min(jnp.where(in_grp & (el == v1), lane, big), axis=-1, keepdims=True)
    rest = in_grp & (lane != i1)
    el2 = jnp.where(rest, lg, NEG_INF)
    v2 = jnp.max(el2, axis=-1, keepdims=True)
    i2 = jnp.min(jnp.where(rest & (el2 == v2), lane, big), axis=-1, keepdims=True)
    d = jnp.exp(v2 - v1)
    p1 = 1.0 / (1.0 + d)
    p2 = d / (1.0 + d)
    out = jnp.where(lane == 0, i1.astype(F32),
                    jnp.where(lane == 1, i2.astype(F32),
                              jnp.where(lane == 2, g_top * p1,
                                        jnp.where(lane == 3, g_top * p2, 0.0))))
    o_ref[...] = out


def moe_route(x, norm_g, w_group, b_group, w_router, b_router, tm=512):
    n = x.shape[0]
    tm = _tile(n, tm)
    padc = LANES - N_EXPERTS - N_GROUPS
    w = jnp.concatenate([w_router, w_group, jnp.zeros((D_MODEL, padc), F32)], axis=1).astype(BF16)
    b = jnp.concatenate([b_router, b_group, jnp.zeros((padc,), F32)]).reshape(1, LANES)
    r = pl.pallas_call(
        _route_kernel,
        grid=(n // tm,),
        in_specs=[pl.BlockSpec((tm, D_MODEL), lambda i: (i, 0)),
                  pl.BlockSpec((1, D_MODEL), lambda i: (0, 0)),
                  pl.BlockSpec((D_MODEL, LANES), lambda i: (0, 0)),
                  pl.BlockSpec((1, LANES), lambda i: (0, 0))],
        out_specs=pl.BlockSpec((tm, LANES), lambda i: (i, 0)),
        out_shape=jax.ShapeDtypeStruct((n, LANES), F32),
        compiler_params=_cparams(("parallel",)),
        name="moe_route",
    )(x, norm_g.reshape(1, D_MODEL), w, b)
    return r[:, :TOP_K].astype(jnp.int32), r[:, TOP_K:2 * TOP_K]


def _dispatch_plan(eid, gate, tm):
    n = eid.shape[0]
    m = n * TOP_K
    flat_e = eid.reshape(-1)
    onehot = (flat_e[:, None] == jnp.arange(N_EXPERTS, dtype=jnp.int32)[None, :]).astype(jnp.int32)
    csum = jnp.cumsum(onehot, axis=0)
    rank = jnp.take_along_axis(csum, flat_e[:, None], axis=1)[:, 0] - 1
    counts = csum[-1]
    padded = (counts + tm - 1) // tm * tm
    pad_end = jnp.cumsum(padded)
    pad_start = pad_end - padded
    dest = pad_start[flat_e] + rank
    p = (m + tm - 1) // tm * tm + N_EXPERTS * tm
    tok_buf = jnp.zeros((p,), jnp.int32).at[dest].set(jnp.arange(m, dtype=jnp.int32) // TOP_K)
    w_buf = jnp.zeros((p,), F32).at[dest].set(gate.reshape(-1))
    n_blocks = p // tm
    block_e = jnp.minimum(jnp.searchsorted(pad_end, jnp.arange(n_blocks, dtype=jnp.int32) * tm, side="right"),
                          N_EXPERTS - 1).astype(jnp.int32)
    n_used = (pad_end[-1] // tm).astype(jnp.int32).reshape(1)
    experts = jnp.arange(N_EXPERTS, dtype=jnp.int32)
    first_nonempty_from = lax.cummin(jnp.where(counts > 0, experts, N_EXPERTS)[::-1])[::-1]
    next_nonempty = jnp.concatenate([first_nonempty_from[1:], jnp.full((1,), N_EXPERTS, jnp.int32)])
    next_e = jnp.where(next_nonempty < N_EXPERTS, next_nonempty, -1)[block_e].astype(jnp.int32)
    return tok_buf, w_buf, block_e, next_e, n_used, dest.reshape(n, TOP_K).astype(jnp.int32)


def _prefetched_rows(idx_ref, idx_next_ref, src_hbm, buf, sems, n_rows):
    i = pl.program_id(0)
    slot = lax.rem(i, 2)

    def row_copy(idx, r, s):
        return pltpu.make_async_copy(src_hbm.at[pl.ds(idx[0, 0, r], 1)], buf.at[s, pl.ds(r, 1)], sems.at[s])

    def start_block(idx, s):
        def body(r2, carry):
            row_copy(idx, 2 * r2, s).start(priority=0)
            row_copy(idx, 2 * r2 + 1, s).start(priority=1)
            return carry
        lax.fori_loop(0, n_rows // 2, body, 0)

    @pl.when(i == 0)
    def _():
        start_block(idx_ref, slot)

    @pl.when(i + 1 < pl.num_programs(0))
    def _():
        start_block(idx_next_ref, 1 - slot)

    def wait_row(r, carry):
        row_copy(idx_ref, r, slot).wait()
        return carry
    lax.fori_loop(0, n_rows, wait_row, 0)
    return slot


def _index_specs(n_rows, nb):
    cur = pl.BlockSpec((1, 1, n_rows), lambda i: (i, 0, 0), memory_space=pltpu.SMEM)
    nxt = pl.BlockSpec((1, 1, n_rows), lambda i: (jnp.minimum(i + 1, nb - 1), 0, 0), memory_space=pltpu.SMEM)
    return [cur, nxt]


def _gather_norm_kernel(tok_ref, tok_next_ref, x_hbm, g_ref, o_ref, buf, sems, *, rows):
    slot = _prefetched_rows(tok_ref, tok_next_ref, x_hbm, buf, sems, rows)
    o_ref[...] = _rms_rows(buf[slot], g_ref[...]).astype(o_ref.dtype)


def moe_gather_norm(x, norm_g, tok_buf, tm):
    p = tok_buf.shape[0]
    nb = p // tm
    tok = tok_buf.reshape(nb, 1, tm)
    return pl.pallas_call(
        functools.partial(_gather_norm_kernel, rows=tm),
        grid=(nb,),
        in_specs=_index_specs(tm, nb) + [pl.BlockSpec(memory_space=pl.ANY),
                                         pl.BlockSpec((1, D_MODEL), lambda i: (0, 0))],
        out_specs=pl.BlockSpec((tm, D_MODEL), lambda i: (i, 0)),
        scratch_shapes=[pltpu.VMEM((2, tm, D_MODEL), F32), pltpu.SemaphoreType.DMA((2,))],
        out_shape=jax.ShapeDtypeStruct((p, D_MODEL), BF16),
        compiler_params=_cparams(("arbitrary",)),
        name="moe_gather_norm",
    )(tok, tok, x, norm_g.reshape(1, D_MODEL))


def _expert_weights(be_ref, nxt_ref, w_hbms, stage, w_bf, sems, *, layer, tn):
    j = pl.program_id(0)
    i = pl.program_id(1)
    e = be_ref[i]

    def copies(expert, jj):
        col = pl.multiple_of(jj * tn, tn)
        return [pltpu.make_async_copy(w.at[layer, expert, :, pl.ds(col, tn)], stage.at[k], sems.at[k])
                for k, w in enumerate(w_hbms)]

    def start(expert, jj):
        for cp in copies(expert, jj):
            cp.start()

    @pl.when((i == 0) | (be_ref[jnp.maximum(i - 1, 0)] != e))
    def _():
        @pl.when((i == 0) & (j == 0))
        def _():
            start(e, j)

        for cp in copies(e, j):
            cp.wait()
        for k in range(len(w_hbms)):
            w_bf[k] = stage[k].astype(BF16)
        nxt = nxt_ref[i]

        @pl.when(nxt >= 0)
        def _():
            start(nxt, j)

        @pl.when((nxt < 0) & (j + 1 < pl.num_programs(0)))
        def _():
            start(be_ref[0], j + 1)


def _moe_up_kernel(be_ref, nxt_ref, nu_ref, xs_ref, wg_hbm, wu_hbm, h_ref, stage, w_bf, sems, *, layer, tn):
    i = pl.program_id(1)

    @pl.when(i < nu_ref[0])
    def _():
        _expert_weights(be_ref, nxt_ref, (wg_hbm, wu_hbm), stage, w_bf, sems, layer=layer, tn=tn)
        x = xs_ref[...]
        g = _dot(x, w_bf[0])
        u = _dot(x, w_bf[1])
        h_ref[...] = (jax.nn.silu(g) * u).astype(h_ref.dtype)

    @pl.when(i >= nu_ref[0])
    def _():
        h_ref[...] = jnp.zeros_like(h_ref)


def moe_up(xs, w_gate, w_up, layer, block_e, next_e, n_used, tm, tn=512):
    p = xs.shape[0]
    nb = p // tm
    clamp = lambda i, nu: jnp.minimum(i, nu[0] - 1)
    return pl.pallas_call(
        functools.partial(_moe_up_kernel, layer=layer, tn=tn),
        grid_spec=pltpu.PrefetchScalarGridSpec(
            num_scalar_prefetch=3,
            grid=(EXPERT_FF // tn, nb),
            in_specs=[pl.BlockSpec((tm, D_MODEL), lambda j, i, be, nx, nu: (clamp(i, nu), 0)),
                      pl.BlockSpec(memory_space=pl.ANY), pl.BlockSpec(memory_space=pl.ANY)],
            out_specs=pl.BlockSpec((tm, tn), lambda j, i, be, nx, nu: (i, j)),
            scratch_shapes=[pltpu.VMEM((2, D_MODEL, tn), F32), pltpu.VMEM((2, D_MODEL, tn), BF16),
                            pltpu.SemaphoreType.DMA((2,))]),
        out_shape=jax.ShapeDtypeStruct((p, EXPERT_FF), BF16),
        compiler_params=_cparams(("arbitrary", "arbitrary")),
        name="moe_up",
    )(block_e, next_e, n_used, xs, w_gate, w_up)


def _moe_down_kernel(be_ref, nxt_ref, nu_ref, h_ref, wd_hbm, w_ref, y_ref, stage, w_bf, sems, *, layer, tn):
    i = pl.program_id(1)

    @pl.when(i < nu_ref[0])
    def _():
        _expert_weights(be_ref, nxt_ref, (wd_hbm,), stage, w_bf, sems, layer=layer, tn=tn)
        y_ref[...] = _dot(h_ref[...], w_bf[0]) * w_ref[...]

    @pl.when(i >= nu_ref[0])
    def _():
        y_ref[...] = jnp.zeros_like(y_ref)


def moe_down(h, w_down, layer, w_buf, block_e, next_e, n_used, tm, tn=2048):
    p = h.shape[0]
    nb = p // tm
    clamp = lambda i, nu: jnp.minimum(i, nu[0] - 1)
    return pl.pallas_call(
        functools.partial(_moe_down_kernel, layer=layer, tn=tn),
        grid_spec=pltpu.PrefetchScalarGridSpec(
            num_scalar_prefetch=3,
            grid=(D_MODEL // tn, nb),
            in_specs=[pl.BlockSpec((tm, EXPERT_FF), lambda j, i, be, nx, nu: (clamp(i, nu), 0)),
                      pl.BlockSpec(memory_space=pl.ANY),
                      pl.BlockSpec((tm, 1), lambda j, i, be, nx, nu: (clamp(i, nu), 0))],
            out_specs=pl.BlockSpec((tm, tn), lambda j, i, be, nx, nu: (i, j)),
            scratch_shapes=[pltpu.VMEM((1, EXPERT_FF, tn), F32), pltpu.VMEM((1, EXPERT_FF, tn), BF16),
                            pltpu.SemaphoreType.DMA((1,))]),
        out_shape=jax.ShapeDtypeStruct((p, D_MODEL), F32),
        compiler_params=_cparams(("arbitrary", "arbitrary")),
        name="moe_down",
    )(block_e, next_e, n_used, h, w_down, w_buf.reshape(p, 1))


def _combine_kernel(pos_ref, pos_next_ref, x_ref, ys_hbm, o_ref, buf, sems, *, tc):
    slot = _prefetched_rows(pos_ref, pos_next_ref, ys_hbm, buf, sems, TOP_K * tc)
    o_ref[...] = x_ref[...] + buf[slot, pl.ds(0, tc)] + buf[slot, pl.ds(tc, tc)]


def moe_combine(x, ys, pos, tc=128):
    n = x.shape[0]
    tc = _tile(n, tc)
    nb = n // tc
    pos_blk = jnp.swapaxes(pos.reshape(nb, tc, TOP_K), 1, 2).reshape(nb, 1, TOP_K * tc)
    blk = pl.BlockSpec((tc, D_MODEL), lambda i: (i, 0))
    return pl.pallas_call(
        functools.partial(_combine_kernel, tc=tc),
        grid=(nb,),
        in_specs=_index_specs(TOP_K * tc, nb) + [blk, pl.BlockSpec(memory_space=pl.ANY)],
        out_specs=blk,
        out_shape=jax.ShapeDtypeStruct((n, D_MODEL), F32),
        scratch_shapes=[pltpu.VMEM((2, TOP_K * tc, D_MODEL), F32), pltpu.SemaphoreType.DMA((2,))],
        compiler_params=_cparams(("arbitrary",)),
        name="moe_combine",
    )(pos_blk, pos_blk, x, ys)


def hier_moe_residual(x, norm_g, layer, w_group, b_group, w_router, b_router, w_gate, w_up, w_down,
                      tm=MOE_BLOCK):
    eid, gate = moe_route(x, norm_g, w_group, b_group, w_router, b_router)
    tok_buf, w_buf, block_e, next_e, n_used, pos = _dispatch_plan(eid, gate, tm)
    xs = moe_gather_norm(x, norm_g, tok_buf, tm)
    h = moe_up(xs, w_gate, w_up, layer, block_e, next_e, n_used, tm)
    ys = moe_down(h, w_down, layer, w_buf, block_e, next_e, n_used, tm)
    return moe_combine(x, ys, pos)


def encoder_layer(x, layer, segs, lb_f, lb_b, norm_mix, w_in, gmlp_v_norm, gmlp_w_s, gmlp_b_s,
                  hgrn_out_norm, conv_w, mla_q_norm, mla_kv_norm, mla_w_uq, mla_w_ukv, qk_norm_q,
                  qk_norm_k, w_branch, w_out, norm_ffn, w_group, b_group, w_router, b_router,
                  w_gate, w_up, w_down):
    w_in_l = w_in[layer]
    w_abc = w_in_l[:, :ABC_WIDTH].astype(BF16)
    w_qkv = jnp.pad(w_in_l[:, ABC_WIDTH:ABC_WIDTH + QKV_WIDTH], ((0, 0), (0, QKV_PAD - QKV_WIDTH))).astype(BF16)
    w_gl = w_in_l[:, ABC_WIDTH + QKV_WIDTH:].astype(BF16)

    xn = rmsnorm_rows(x, norm_mix[layer])
    pabc = matmul(xn, w_abc, out_dtype=F32, tm=1024, tn=1024, name="proj_abc")
    pqkv = matmul(xn, w_qkv, out_dtype=F32, tm=1024, tn=384, name="proj_qkv")
    gates = matmul(xn, w_gl, out_dtype=BF16, tm=1024, tn=1024, act="sigmoid", name="proj_gates")

    branches = (
        gmlp_mixer(pabc, gmlp_v_norm[layer], gmlp_w_s[layer], gmlp_b_s[layer]),
        hgrn_mixer(pabc, lb_f, lb_b, hgrn_out_norm[layer], segs),
        conv_mixer(pabc, conv_w[layer], segs),
        mla_mixer(pqkv, mla_q_norm[layer], mla_kv_norm[layer], mla_w_uq[layer], mla_w_ukv[layer],
                  qk_norm_q[layer], qk_norm_k[layer], segs),
    )
    merged = merge_branches(branches, w_branch[layer].astype(BF16), gates)
    x = matmul(merged, w_out[layer].astype(BF16), out_dtype=F32, tm=1024, tn=1024, res=x, name="out_proj")
    return hier_moe_residual(x, norm_ffn[layer], layer, w_group[layer], b_group[layer], w_router[layer],
                             b_router[layer], w_gate, w_up, w_down)


def kernel(x_prompt, x_sample, norm_mix, w_in, gmlp_v_norm, gmlp_w_s, gmlp_b_s, hgrn_lb_logits, hgrn_out_norm, conv_w, mla_q_norm, mla_kv_norm, mla_w_uq, mla_w_ukv, qk_norm_q, qk_norm_k, w_branch, w_out, norm_ffn, w_group, b_group, w_router, b_router, w_gate, w_up, w_down):
    depth = norm_mix.shape[0]
    bp, sp, d = x_prompt.shape
    bs, ss, _ = x_sample.shape
    n_p, n_s = bp * sp, bs * ss
    segs = ((0, n_p, sp), (n_p, n_s, ss))
    x = jnp.concatenate([x_prompt.reshape(n_p, d), x_sample.reshape(n_s, d)], axis=0)

    lb_p = jax.nn.softmax(hgrn_lb_logits.astype(F32), axis=1)
    lb = jnp.maximum(jnp.cumsum(lb_p, axis=1) - lb_p[:, :1], 0.0)

    for layer in range(depth):
        x = encoder_layer(x, layer, segs, lb[0, layer], lb[1, layer], norm_mix, w_in, gmlp_v_norm, gmlp_w_s,
                          gmlp_b_s, hgrn_out_norm, conv_w, mla_q_norm, mla_kv_norm, mla_w_uq, mla_w_ukv,
                          qk_norm_q, qk_norm_k, w_branch, w_out, norm_ffn, w_group, b_group, w_router,
                          b_router, w_gate, w_up, w_down)
    return (x[:n_p].reshape(bp, sp, d), x[n_p:].reshape(bs, ss, d))
```

```python
import functools
import math

import jax
import jax.numpy as jnp
import numpy as np
from jax import lax
from jax.experimental import pallas as pl
from jax.experimental.pallas import tpu as pltpu

F32 = jnp.float32
BF16 = jnp.bfloat16

D_MODEL = 4096
BRANCH_WIDTH = 1024
HEAD_DIM = 128
N_HEADS = 8
GMLP_CHUNK = 128
MLA_Q_RANK = 768
MLA_KV_RANK = 256
MLA_NOPE = 128
MLA_ROPE = 64
MLA_QK = MLA_NOPE + MLA_ROPE
MLA_QK_PAD = 256
ROPE_BASE = 10000.0
N_GROUPS = 8
EXPERTS_PER_GROUP = 8
N_EXPERTS = 64
TOP_K = 2
EXPERT_FF = 1024
EPS = 1e-6
ABC_WIDTH = 10 * BRANCH_WIDTH
QKV_WIDTH = MLA_Q_RANK + MLA_KV_RANK + MLA_ROPE
QKV_PAD = 1152
LANES = 128
VMEM_LIMIT = 56 * 1024 * 1024
HGRN_CHUNK = 64
MOE_BLOCK = 256
NEG_INF = float("-inf")


def _cparams(sem):
    return pltpu.CompilerParams(dimension_semantics=sem, vmem_limit_bytes=VMEM_LIMIT)


def _tile(n, t):
    t = min(t, n)
    while n % t:
        t //= 2
    return t


def _dot(a, b):
    return jnp.dot(a, b, preferred_element_type=F32)


def _dot_nt(a, b):
    return lax.dot_general(a, b, (((1,), (1,)), ((), ())), preferred_element_type=F32)


def _rmsnorm_kernel(x_ref, g_ref, o_ref):
    x = x_ref[...]
    ms = jnp.mean(x * x, axis=-1, keepdims=True)
    o_ref[...] = (x * lax.rsqrt(ms + EPS) * g_ref[...]).astype(o_ref.dtype)


def rmsnorm_rows(x, g, tm=512):
    n, d = x.shape
    tm = _tile(n, tm)
    return pl.pallas_call(
        _rmsnorm_kernel,
        grid=(n // tm,),
        in_specs=[pl.BlockSpec((tm, d), lambda i: (i, 0)),
                  pl.BlockSpec((1, d), lambda i: (0, 0))],
        out_specs=pl.BlockSpec((tm, d), lambda i: (i, 0)),
        out_shape=jax.ShapeDtypeStruct((n, d), BF16),
        compiler_params=_cparams(("parallel",)),
        name="rmsnorm",
    )(x, g.reshape(1, d))


def _mm_kernel(x_ref, w_ref, *rest, act, has_res):
    o_ref = rest[-1]
    acc = _dot(x_ref[...], w_ref[...])
    if act == "sigmoid":
        acc = jax.nn.sigmoid(acc)
    if has_res:
        acc = acc + rest[0][...]
    o_ref[...] = acc.astype(o_ref.dtype)


def matmul(x, w, *, out_dtype, tm, tn, act=None, res=None, name="matmul"):
    m, k = x.shape
    n = w.shape[1]
    tm = _tile(m, tm)
    in_specs = [pl.BlockSpec((tm, k), lambda i, j: (i, 0)),
                pl.BlockSpec((k, tn), lambda i, j: (0, j))]
    args = [x, w]
    if res is not None:
        in_specs.append(pl.BlockSpec((tm, tn), lambda i, j: (i, j)))
        args.append(res)
    return pl.pallas_call(
        functools.partial(_mm_kernel, act=act, has_res=res is not None),
        grid=(m // tm, n // tn),
        in_specs=in_specs,
        out_specs=pl.BlockSpec((tm, tn), lambda i, j: (i, j)),
        out_shape=jax.ShapeDtypeStruct((m, n), out_dtype),
        compiler_params=_cparams(("parallel", "parallel")),
        name=name,
    )(*args)


def _merge_kernel(b0, b1, b2, b3, wb_ref, g0, g1, g2, g3, o_ref):
    acc = None
    for kb, (b_ref, g_ref) in enumerate(((b0, g0), (b1, g1), (b2, g2), (b3, g3))):
        t = _dot(b_ref[...], wb_ref[kb]) * g_ref[...].astype(F32)
        acc = t if acc is None else acc + t
    o_ref[...] = acc.astype(o_ref.dtype)


def merge_branches(branches, w_branch, gates, tm=1024, tn=512):
    n = branches[0].shape[0]
    tm = _tile(n, tm)
    nj = D_MODEL // tn
    b_spec = pl.BlockSpec((tm, BRANCH_WIDTH), lambda i, j: (i, 0))
    g_specs = [pl.BlockSpec((tm, tn), functools.partial(lambda i, j, kb: (i, kb * nj + j), kb=kb))
               for kb in range(4)]
    return pl.pallas_call(
        _merge_kernel,
        grid=(n // tm, nj),
        in_specs=[b_spec] * 4 + [pl.BlockSpec((4, BRANCH_WIDTH, tn), lambda i, j: (0, 0, j))] + g_specs,
        out_specs=pl.BlockSpec((tm, tn), lambda i, j: (i, j)),
        out_shape=jax.ShapeDtypeStruct((n, D_MODEL), BF16),
        compiler_params=_cparams(("parallel", "parallel")),
        name="merge_branches",
    )(*branches, w_branch, gates, gates, gates, gates)


def _gmlp_kernel(u_ref, v_ref, g_ref, ws_ref, bs_ref, o_ref, *, tm):
    for c in range(tm // GMLP_CHUNK):
        rows = pl.ds(c * GMLP_CHUNK, GMLP_CHUNK)
        u = jax.nn.gelu(u_ref[rows, :])
        v = jax.nn.gelu(v_ref[rows, :])
        ms = jnp.mean(v * v, axis=-1, keepdims=True)
        vb = (v * lax.rsqrt(ms + EPS) * g_ref[...]).astype(BF16)
        for h in range(N_HEADS):
            cols = slice(h * HEAD_DIM, (h + 1) * HEAD_DIM)
            mixed = _dot(ws_ref[h], vb[:, cols]) + bs_ref[:, cols]
            o_ref[rows, cols] = (u[:, cols] * mixed).astype(o_ref.dtype)


def gmlp_mixer(pabc, v_gain, w_s, b_s, tm=512):
    n = pabc.shape[0]
    tm = _tile(n, tm)
    bias = jnp.repeat(jnp.swapaxes(b_s, 0, 1), HEAD_DIM, axis=1)
    return pl.pallas_call(
        functools.partial(_gmlp_kernel, tm=tm),
        grid=(n // tm,),
        in_specs=[pl.BlockSpec((tm, BRANCH_WIDTH), lambda i: (i, 0)),
                  pl.BlockSpec((tm, BRANCH_WIDTH), lambda i: (i, 1)),
                  pl.BlockSpec((1, BRANCH_WIDTH), lambda i: (0, 0)),
                  pl.BlockSpec((N_HEADS, GMLP_CHUNK, GMLP_CHUNK), lambda i: (0, 0, 0)),
                  pl.BlockSpec((GMLP_CHUNK, BRANCH_WIDTH), lambda i: (0, 0))],
        out_specs=pl.BlockSpec((tm, BRANCH_WIDTH), lambda i: (i, 0)),
        out_shape=jax.ShapeDtypeStruct((n, BRANCH_WIDTH), BF16),
        compiler_params=_cparams(("parallel",)),
        name="gmlp_mixer",
    )(pabc, pabc, v_gain.reshape(1, -1), w_s.astype(BF16), bias)


def _segment_of(row0, segs):
    seq_len = jnp.int32(segs[-1][2])
    first = jnp.int32(segs[-1][0])
    for (f, nr, s) in reversed(segs[:-1]):
        inside = row0 < f + nr
        seq_len = jnp.where(inside, s, seq_len)
        first = jnp.where(inside, f, first)
    return seq_len, first


def _conv_kernel(bg_ref, cg_ref, h_ref, cgp_ref, hp_ref, cgn_ref, hn_ref, w_ref, o_ref, *, tm, segs):
    row0 = pl.program_id(0) * tm
    seq_len, first = _segment_of(row0, segs)
    at_start = lax.rem(row0 - first, seq_len) == 0
    at_end = lax.rem(row0 - first + tm, seq_len) == 0
    z = cg_ref[...] * h_ref[...]
    left = jnp.where(at_start, 0.0, cgp_ref[7:8, :] * hp_ref[7:8, :])
    right = jnp.where(at_end, 0.0, cgn_ref[0:1, :] * hn_ref[0:1, :])
    ridx = lax.broadcasted_iota(jnp.int32, z.shape, 0)
    z_prev = jnp.where(ridx == 0, left, pltpu.roll(z, 1, axis=0))
    z_next = jnp.where(ridx == tm - 1, right, pltpu.roll(z, tm - 1, axis=0))
    y = w_ref[0:1, :] * z_prev + w_ref[1:2, :] * z + w_ref[2:3, :] * z_next
    o_ref[...] = (bg_ref[...] * y).astype(o_ref.dtype)


def conv_mixer(pabc, conv_w, segs, tm=256):
    n = pabc.shape[0]
    tm = _tile(n, tm)
    nb8 = n // 8
    blk = lambda c: pl.BlockSpec((tm, BRANCH_WIDTH), lambda i: (i, c))
    prev8 = lambda c: pl.BlockSpec((8, BRANCH_WIDTH), lambda i: (jnp.maximum(i * (tm // 8) - 1, 0), c))
    next8 = lambda c: pl.BlockSpec((8, BRANCH_WIDTH), lambda i: (jnp.minimum((i + 1) * (tm // 8), nb8 - 1), c))
    return pl.pallas_call(
        functools.partial(_conv_kernel, tm=tm, segs=segs),
        grid=(n // tm,),
        in_specs=[blk(7), blk(8), blk(9), prev8(8), prev8(9), next8(8), next8(9),
                  pl.BlockSpec((3, BRANCH_WIDTH), lambda i: (0, 0))],
        out_specs=pl.BlockSpec((tm, BRANCH_WIDTH), lambda i: (i, 0)),
        out_shape=jax.ShapeDtypeStruct((n, BRANCH_WIDTH), BF16),
        compiler_params=_cparams(("parallel",)),
        name="conv_mixer",
    )(pabc, pabc, pabc, pabc, pabc, pabc, pabc, conv_w)


def _hgrn_constants(c, reverse):
    n_lev = int(math.log2(c))
    t = np.arange(c)[:, None]
    u = np.arange(c)[None, :]
    if not reverse:
        w_a = u <= t
        w_kd = u > t
    else:
        w_a = u >= t
        w_kd = u < t
    ws = [w_a, w_kd]
    masks = [np.eye(c, dtype=bool)]
    for lev in range(n_lev):
        m = c >> (lev + 1)
        blk_t = t // (2 * m)
        upper_t = (t % (2 * m)) >= m
        p_last_lower = blk_t * 2 * m + m - 1
        if not reverse:
            w_q = (u > p_last_lower) & (u <= t)
            w_k = (u > t) & (u <= p_last_lower)
            w = np.where(upper_t, w_q, w_k)
            query_t = upper_t
        else:
            p_first_upper = p_last_lower + 1
            w_q = (u >= t) & (u < p_first_upper)
            w_k = (u >= p_first_upper) & (u < t)
            w = np.where(upper_t, w_k, w_q)
            query_t = ~upper_t
        ws.append(w)
        same_blk = blk_t == blk_t.T
        masks.append(same_blk & query_t & (~query_t).T)
    return np.stack(ws).astype(np.float32), np.stack(masks).astype(np.float32)


def _hgrn_pair_constants(c):
    ws_f, masks_f = _hgrn_constants(c, False)
    ws_b, masks_b = _hgrn_constants(c, True)

    def blockdiag(a, b):
        out = np.zeros((a.shape[0], 2 * c, 2 * c), np.float32)
        out[:, :c, :c] = a
        out[:, c:, c:] = b
        return out

    w_all = blockdiag(ws_f, ws_b).reshape(-1, 2 * c)
    return jnp.asarray(w_all, BF16), jnp.asarray(blockdiag(masks_f, masks_b), F32)


def _log_forget_and_key(z, log_lb, log1m_lb):
    log_sig = jnp.minimum(z, 0.0) - jnp.log1p(jnp.exp(-jnp.abs(z)))
    x2 = log1m_lb + log_sig
    hi = jnp.maximum(log_lb, x2)
    lf = hi + jnp.log1p(jnp.exp(-jnp.abs(log_lb - x2)))
    return lf, jnp.exp(x2 - z)


def _split3(x):
    p1 = x.astype(BF16)
    r1 = x - p1.astype(F32)
    p2 = r1.astype(BF16)
    p3 = (r1 - p2.astype(F32)).astype(BF16)
    return p1, p2, p3


def _hgrn_kernel(qf_ref, vf_ref, zf_ref, qb_ref, vb_ref, zb_ref, lb_ref, wall_ref, mask_ref,
                 of_ref, ob_ref, st_ref, *, c, n_lev, segs):
    step = pl.program_id(0)
    row_f = step * c
    row_b = (pl.num_programs(0) - 1 - step) * c
    len_f, first_f = _segment_of(row_f, segs)
    len_b, first_b = _segment_of(row_b, segs)

    @pl.when(lax.rem(row_f - first_f, len_f) == 0)
    def _():
        st_ref[0] = jnp.zeros(st_ref.shape[1:], F32)

    @pl.when(lax.rem(row_b - first_b + c, len_b) == 0)
    def _():
        st_ref[1] = jnp.zeros(st_ref.shape[1:], F32)

    w_all = wall_ref[...]
    q = jnp.concatenate([qf_ref[...], qb_ref[...]], axis=0)
    v = jnp.concatenate([vf_ref[...], vb_ref[...]], axis=0)
    z = jnp.concatenate([zf_ref[...], zb_ref[...]], axis=0)
    row = lax.broadcasted_iota(jnp.int32, q.shape, 0)
    is_fwd = row < c
    lf, k = _log_forget_and_key(z, jnp.where(is_fwd, lb_ref[0:1, :], lb_ref[2:3, :]),
                                jnp.where(is_fwd, lb_ref[1:2, :], lb_ref[3:4, :]))
    p1, p2, p3 = _split3(lf)
    e_all = _dot(w_all, p1) + _dot(w_all, p2) + _dot(w_all, p3)
    c2 = 2 * c
    a = e_all[0:c2]
    qd = (q * jnp.exp(a)).astype(BF16)
    kd = (k * jnp.exp(e_all[c2:2 * c2])).astype(BF16)
    q16 = q.astype(BF16)
    k16 = k.astype(BF16)
    v16 = v.astype(BF16)
    xms = []
    for lev in range(n_lev):
        upper = ((row >> (n_lev - 1 - lev)) & 1) == 1
        is_query = upper == is_fwd
        dm = jnp.exp(e_all[(2 + lev) * c2:(3 + lev) * c2])
        xms.append((jnp.where(is_query, q, k) * dm).astype(BF16))
    decay_f = jnp.exp(a[c - 1:c])
    decay_b = jnp.exp(a[c:c + 1])
    for h in range(N_HEADS):
        cols = slice(h * HEAD_DIM, (h + 1) * HEAD_DIM)
        scores = mask_ref[0] * _dot_nt(q16[:, cols], k16[:, cols])
        for lev in range(n_lev):
            xm = xms[lev][:, cols]
            scores = scores + mask_ref[lev + 1] * _dot_nt(xm, xm)
        o_intra = _dot(scores.astype(BF16), v16[:, cols])
        st_f = st_ref[0, h]
        st_b = st_ref[1, h]
        of_ref[:, cols] = o_intra[:c] + _dot_nt(qd[:c, cols], st_f.astype(BF16))
        ob_ref[:, cols] = o_intra[c:] + _dot_nt(qd[c:, cols], st_b.astype(BF16))
        st_ref[0, h] = st_f * decay_f[:, cols] + _dot(v[:c, cols].T.astype(BF16), kd[:c, cols])
        st_ref[1, h] = st_b * decay_b[:, cols] + _dot(v[c:, cols].T.astype(BF16), kd[c:, cols])


def _hgrn_out_kernel(of_ref, ob_ref, g_ref, gain_ref, o_ref):
    for h in range(N_HEADS):
        cols = slice(h * HEAD_DIM, (h + 1) * HEAD_DIM)
        o = of_ref[:, cols] + ob_ref[:, cols]
        ms = jnp.mean(o * o, axis=-1, keepdims=True)
        o_ref[:, cols] = (o * lax.rsqrt(ms + EPS) * gain_ref[...] * jax.nn.silu(g_ref[:, cols])).astype(o_ref.dtype)


def hgrn_mixer(pabc, lb_f, lb_b, out_gain, segs, c=HGRN_CHUNK, tm=512):
    n = pabc.shape[0]
    n_lev = int(math.log2(c))
    n_chunks = n // c
    w_all, masks = _hgrn_pair_constants(c)
    log_lb = jnp.stack([jnp.log(lb_f), jnp.log1p(-lb_f), jnp.log(lb_b), jnp.log1p(-lb_b)])
    fwd = lambda col: pl.BlockSpec((c, BRANCH_WIDTH), lambda i: (i, col))
    bwd = lambda col: pl.BlockSpec((c, BRANCH_WIDTH), lambda i: (n_chunks - 1 - i, col))
    o_f, o_b = pl.pallas_call(
        functools.partial(_hgrn_kernel, c=c, n_lev=n_lev, segs=segs),
        grid=(n_chunks,),
        in_specs=[fwd(2), fwd(3), fwd(4), bwd(2), bwd(3), bwd(5),
                  pl.BlockSpec((4, BRANCH_WIDTH), lambda i: (0, 0)),
                  pl.BlockSpec(w_all.shape, lambda i: (0, 0)),
                  pl.BlockSpec(masks.shape, lambda i: (0, 0, 0))],
        out_specs=[fwd(0), bwd(0)],
        out_shape=[jax.ShapeDtypeStruct((n, BRANCH_WIDTH), F32)] * 2,
        scratch_shapes=[pltpu.VMEM((2, N_HEADS, HEAD_DIM, HEAD_DIM), F32)],
        compiler_params=_cparams(("arbitrary",)),
        name="hgrn_scan",
    )(pabc, pabc, pabc, pabc, pabc, pabc, log_lb, w_all, masks)
    tm = _tile(n, tm)
    blk = lambda col: pl.BlockSpec((tm, BRANCH_WIDTH), lambda i: (i, col))
    return pl.pallas_call(
        _hgrn_out_kernel,
        grid=(n // tm,),
        in_specs=[blk(0), blk(0), blk(6), pl.BlockSpec((1, HEAD_DIM), lambda i: (0, 0))],
        out_specs=blk(0),
        out_shape=jax.ShapeDtypeStruct((n, BRANCH_WIDTH), BF16),
        compiler_params=_cparams(("parallel",)),
        name="hgrn_out",
    )(o_f, o_b, pabc, out_gain.reshape(1, HEAD_DIM))


def _rope_tile(r, cos_t, sin_a, sin_b):
    return r * cos_t + pltpu.roll(r, 96, axis=1) * sin_a + pltpu.roll(r, 32, axis=1) * sin_b


def _mla_prep_kernel(cq_ref, ckv_ref, wq_ref, wkv_ref, qg_ref, kvg_ref, gq_ref, gk_ref,
                     cos_ref, sina_ref, sinb_ref, q_out, k_out, v_out):
    cq = cq_ref[...]
    hq = (cq * lax.rsqrt(jnp.mean(cq * cq, axis=-1, keepdims=True) + EPS) * qg_ref[...]).astype(BF16)
    qf = _dot(hq, wq_ref[...])
    ckvkr = ckv_ref[...]
    ckv = ckvkr[:, :MLA_KV_RANK]
    kr = ckvkr[:, MLA_KV_RANK:]
    hk = (ckv * lax.rsqrt(jnp.mean(ckv * ckv, axis=-1, keepdims=True) + EPS) * kvg_ref[...]).astype(BF16)
    kvf = _dot(hk, wkv_ref[...])
    cos_t, sin_a, sin_b = cos_ref[...], sina_ref[...], sinb_ref[...]
    gq_n, gq_r = gq_ref[:, :MLA_NOPE], gq_ref[:, MLA_NOPE:]
    gk_n, gk_r = gk_ref[:, :MLA_NOPE], gk_ref[:, MLA_NOPE:]
    kr_ss = jnp.sum(kr * kr, axis=-1, keepdims=True)
    kr_rot = _rope_tile(kr * gk_r, cos_t, sin_a, sin_b)
    scale = MLA_QK ** -0.5 * math.log2(math.e)
    lane = lax.broadcasted_iota(jnp.int32, (cq.shape[0], HEAD_DIM), 1)
    ones_col = jnp.where(lane == 0, 1.0, 0.0).astype(BF16)
    for h in range(N_HEADS):
        base = h * MLA_QK_PAD
        qn = qf[:, base:base + MLA_NOPE]
        qr = qf[:, base + MLA_NOPE:base + MLA_QK_PAD]
        ss = jnp.sum(qn * qn, axis=-1, keepdims=True) + jnp.sum(qr * qr, axis=-1, keepdims=True)
        rstd = lax.rsqrt(ss * (1.0 / MLA_QK) + EPS) * scale
        q_out[h, :, :MLA_NOPE] = (qn * rstd * gq_n).astype(BF16)
        q_out[h, :, MLA_NOPE:] = _rope_tile(qr * rstd * gq_r, cos_t, sin_a, sin_b).astype(BF16)
        kn = kvf[:, base:base + MLA_NOPE]
        ssk = jnp.sum(kn * kn, axis=-1, keepdims=True) + kr_ss
        rstdk = lax.rsqrt(ssk * (1.0 / MLA_QK) + EPS)
        k_out[h, :, :MLA_NOPE] = (kn * rstdk * gk_n).astype(BF16)
        k_out[h, :, MLA_NOPE:] = (kr_rot * rstdk).astype(BF16)
        v_out[h, :, :HEAD_DIM] = kvf[:, base + MLA_NOPE:base + MLA_QK_PAD].astype(BF16)
        v_out[h, :, HEAD_DIM:] = ones_col


def _rope_tables(segs):
    half = MLA_ROPE // 2
    inv_freq = ROPE_BASE ** (-jnp.arange(half, dtype=F32) * 2.0 / MLA_ROPE)
    pos = jnp.concatenate([jnp.tile(jnp.arange(s, dtype=F32), nr // s) for (_, nr, s) in segs])
    ang = pos[:, None] * inv_freq[None, :]
    cos, sin = jnp.cos(ang), jnp.sin(ang)
    zero = jnp.zeros_like(cos)
    zero2 = jnp.concatenate([zero, zero], axis=1)
    cos_t = jnp.concatenate([cos, cos, zero2], axis=1)
    sin_a = jnp.concatenate([-sin, zero, zero2], axis=1)
    sin_b = jnp.concatenate([zero, sin, zero2], axis=1)
    return cos_t, sin_a, sin_b


def mla_prep(pqkv, q_gain, kv_gain, w_uq, w_ukv, gq, gk, segs, tm=256):
    n = pqkv.shape[0]
    tm = _tile(n, tm)
    wq = jnp.pad(w_uq.reshape(MLA_Q_RANK, N_HEADS, MLA_QK),
                 ((0, 0), (0, 0), (0, MLA_QK_PAD - MLA_QK))).reshape(MLA_Q_RANK, -1).astype(BF16)
    wkv = w_ukv.astype(BF16)
    pad = lambda g: jnp.pad(g, (0, MLA_QK_PAD - MLA_QK)).reshape(1, MLA_QK_PAD)
    cos_t, sin_a, sin_b = _rope_tables(segs)
    const = lambda shape: pl.BlockSpec(shape, lambda i: (0, 0))
    tab = pl.BlockSpec((tm, LANES), lambda i: (i, 0))
    hd = lambda w: pl.BlockSpec((N_HEADS, tm, w), lambda i: (0, i, 0))
    return pl.pallas_call(
        _mla_prep_kernel,
        grid=(n // tm,),
        in_specs=[pl.BlockSpec((tm, MLA_Q_RANK), lambda i: (i, 0)),
                  pl.BlockSpec((tm, QKV_PAD - MLA_Q_RANK), lambda i: (i, 2)),
                  const(wq.shape), const(wkv.shape),
                  const((1, MLA_Q_RANK)), const((1, MLA_KV_RANK)),
                  const((1, MLA_QK_PAD)), const((1, MLA_QK_PAD)), tab, tab, tab],
        out_specs=[hd(MLA_QK_PAD), hd(MLA_QK_PAD), hd(2 * HEAD_DIM)],
        out_shape=[jax.ShapeDtypeStruct((N_HEADS, n, MLA_QK_PAD), BF16),
                   jax.ShapeDtypeStruct((N_HEADS, n, MLA_QK_PAD), BF16),
                   jax.ShapeDtypeStruct((N_HEADS, n, 2 * HEAD_DIM), BF16)],
        compiler_params=_cparams(("parallel",)),
        name="mla_prep",
    )(pqkv, pqkv, wq, wkv, q_gain.reshape(1, -1), kv_gain.reshape(1, -1), pad(gq), pad(gk),
      cos_t, sin_a, sin_b)


def _attn_kernel(q_ref, k_ref, v_ref, o_ref, *, kv_chunk):
    q = q_ref[...]
    acc = None
    m_run = None
    for c0 in range(0, k_ref.shape[0], kv_chunk):
        s = _dot_nt(q, k_ref[pl.ds(c0, kv_chunk), :])
        m = jnp.max(s, axis=-1, keepdims=True)
        ol = _dot(jnp.exp2(s - m).astype(BF16), v_ref[pl.ds(c0, kv_chunk), :])
        if acc is None:
            acc, m_run = ol, m
        else:
            m_new = jnp.maximum(m_run, m)
            acc = acc * jnp.exp2(m_run - m_new) + ol * jnp.exp2(m - m_new)
            m_run = m_new
    o_ref[...] = (acc[:, :HEAD_DIM] / acc[:, HEAD_DIM:HEAD_DIM + 1]).astype(o_ref.dtype)


def attention_segment(q, k, v, first_row, n_rows, seq_len, tq=512, kv_chunk=1024):
    tq = min(tq, seq_len)
    kv_chunk = min(kv_chunk, seq_len)
    n_seq = n_rows // seq_len
    qb_per_seq = seq_len // tq
    qoff = first_row // tq
    soff = first_row // seq_len
    return pl.pallas_call(
        functools.partial(_attn_kernel, kv_chunk=kv_chunk),
        grid=(n_seq, N_HEADS, qb_per_seq),
        in_specs=[pl.BlockSpec((None, tq, MLA_QK_PAD), lambda b, h, i: (h, qoff + b * qb_per_seq + i, 0)),
                  pl.BlockSpec((None, seq_len, MLA_QK_PAD), lambda b, h, i: (h, soff + b, 0)),
                  pl.BlockSpec((None, seq_len, 2 * HEAD_DIM), lambda b, h, i: (h, soff + b, 0))],
        out_specs=pl.BlockSpec((tq, HEAD_DIM), lambda b, h, i: (b * qb_per_seq + i, h)),
        out_shape=jax.ShapeDtypeStruct((n_rows, BRANCH_WIDTH), BF16),
        compiler_params=_cparams(("parallel", "parallel", "parallel")),
        name="mla_attention",
    )(q, k, v)


def mla_mixer(pqkv, q_gain, kv_gain, w_uq, w_ukv, gq, gk, segs):
    q, k, v = mla_prep(pqkv, q_gain, kv_gain, w_uq, w_ukv, gq, gk, segs)
    outs = [attention_segment(q, k, v, f, nr, s) for (f, nr, s) in segs]
    return outs[0] if len(outs) == 1 else jnp.concatenate(outs, axis=0)


def _rms_rows(x, g):
    return x * lax.rsqrt(jnp.mean(x * x, axis=-1, keepdims=True) + EPS) * g


def _route_kernel(x_ref, g_ref, w_ref, b_ref, o_ref):
    xn = _rms_rows(x_ref[...], g_ref[...]).astype(BF16)
    lg = _dot(xn, w_ref[...]) + b_ref[...]
    lane = lax.broadcasted_iota(jnp.int32, lg.shape, 1)
    big = jnp.int32(1 << 20)
    is_g = (lane >= N_EXPERTS) & (lane < N_EXPERTS + N_GROUPS)
    gl = jnp.where(is_g, lg, NEG_INF)
    gmax = jnp.max(gl, axis=-1, keepdims=True)
    gidx = jnp.min(jnp.where(is_g & (gl == gmax), lane - N_EXPERTS, big), axis=-1, keepdims=True)
    gsum = jnp.sum(jnp.where(is_g, jnp.exp(gl - gmax), 0.0), axis=-1, keepdims=True)
    g_top = 1.0 / gsum
    in_grp = (lane < N_EXPERTS) & ((lane >> 3) == gidx)
    el = jnp.where(in_grp, lg, NEG_INF)
    v1 = jnp.max(el, axis=-1, keepdims=True)
    i1 = jnp.min(jnp.where(in_grp & (el == v1), lane, big), axis=-1, keepdims=True)
    rest = in_grp & (lane != i1)
    el2 = jnp.where(rest, lg, NEG_INF)
    v2 = jnp.max(el2, axis=-1, keepdims=True)
    i2 = jnp.min(jnp.where(rest & (el2 == v2), lane, big), axis=-1, keepdims=True)
    d = jnp.exp(v2 - v1)
    p1 = 1.0 / (1.0 + d)
    p2 = d / (1.0 + d)
    out = jnp.where(lane == 0, i1.astype(F32),
                    jnp.where(lane == 1, i2.astype(F32),
                              jnp.where(lane == 2, g_top * p1,
                                        jnp.where(lane == 3, g_top * p2, 0.0))))
    o_ref[...] = out


def moe_route(x, norm_g, w_group, b_group, w_router, b_router, tm=512):
    n = x.shape[0]
    tm = _tile(n, tm)
    padc = LANES - N_EXPERTS - N_GROUPS
    w = jnp.concatenate([w_router, w_group, jnp.zeros((D_MODEL, padc), F32)], axis=1).astype(BF16)
    b = jnp.concatenate([b_router, b_group, jnp.zeros((padc,), F32)]).reshape(1, LANES)
    r = pl.pallas_call(
        _route_kernel,
        grid=(n // tm,),
        in_specs=[pl.BlockSpec((tm, D_MODEL), lambda i: (i, 0)),
                  pl.BlockSpec((1, D_MODEL), lambda i: (0, 0)),
                  pl.BlockSpec((D_MODEL, LANES), lambda i: (0, 0)),
                  pl.BlockSpec((1, LANES), lambda i: (0, 0))],
        out_specs=pl.BlockSpec((tm, LANES), lambda i: (i, 0)),
        out_shape=jax.ShapeDtypeStruct((n, LANES), F32),
        compiler_params=_cparams(("parallel",)),
        name="moe_route",
    )(x, norm_g.reshape(1, D_MODEL), w, b)
    return r[:, :TOP_K].astype(jnp.int32), r[:, TOP_K:2 * TOP_K]


def _dispatch_plan(eid, tm):
    n = eid.shape[0]
    m = n * TOP_K
    flat_e = eid.reshape(-1)
    onehot = (flat_e[:, None] == jnp.arange(N_EXPERTS, dtype=jnp.int32)[None, :]).astype(jnp.int32)
    csum = jnp.cumsum(onehot, axis=0)
    rank = jnp.take_along_axis(csum, flat_e[:, None], axis=1)[:, 0] - 1
    counts = csum[-1]
    padded = (counts + tm - 1) // tm * tm
    pad_end = jnp.cumsum(padded)
    pad_start = pad_end - padded
    dest = pad_start[flat_e] + rank
    p = (m + tm - 1) // tm * tm + N_EXPERTS * tm
    tok_buf = jnp.full((p,), -1, jnp.int32).at[dest].set(jnp.arange(m, dtype=jnp.int32) // TOP_K)
    n_blocks = p // tm
    block_e = jnp.minimum(jnp.searchsorted(pad_end, jnp.arange(n_blocks, dtype=jnp.int32) * tm, side="right"),
                          N_EXPERTS - 1).astype(jnp.int32)
    n_used = (pad_end[-1] // tm).astype(jnp.int32).reshape(1)
    experts = jnp.arange(N_EXPERTS, dtype=jnp.int32)
    first_nonempty_from = lax.cummin(jnp.where(counts > 0, experts, N_EXPERTS)[::-1])[::-1]
    next_nonempty = jnp.concatenate([first_nonempty_from[1:], jnp.full((1,), N_EXPERTS, jnp.int32)])
    next_e = jnp.where(next_nonempty < N_EXPERTS, next_nonempty, -1)[block_e].astype(jnp.int32)
    return tok_buf, block_e, next_e, n_used, dest.reshape(n, TOP_K).astype(jnp.int32)


def _prefetched_rows(idx_ref, idx_next_ref, src_hbm, buf, sems, n_rows):
    i = pl.program_id(0)
    slot = lax.rem(i, 2)

    def row_copy(idx, r, s):
        return pltpu.make_async_copy(src_hbm.at[pl.ds(idx[0, 0, r], 1)], buf.at[s, pl.ds(r, 1)], sems.at[s])

    def start_block(idx, s):
        def body(r2, carry):
            for k in range(2):
                r = 2 * r2 + k

                @pl.when(idx[0, 0, r] >= 0)
                def _():
                    row_copy(idx, r, s).start(priority=k)
            return carry
        lax.fori_loop(0, n_rows // 2, body, 0)

    @pl.when(i == 0)
    def _():
        buf[...] = jnp.zeros_like(buf)
        start_block(idx_ref, slot)

    @pl.when(i + 1 < pl.num_programs(0))
    def _():
        start_block(idx_next_ref, 1 - slot)

    def wait_row(r, carry):
        @pl.when(idx_ref[0, 0, r] >= 0)
        def _():
            row_copy(idx_ref, r, slot).wait()
        return carry
    lax.fori_loop(0, n_rows, wait_row, 0)
    return slot


def _index_specs(n_rows, nb):
    cur = pl.BlockSpec((1, 1, n_rows), lambda i: (i, 0, 0), memory_space=pltpu.SMEM)
    nxt = pl.BlockSpec((1, 1, n_rows), lambda i: (jnp.minimum(i + 1, nb - 1), 0, 0), memory_space=pltpu.SMEM)
    return [cur, nxt]


def _gather_norm_kernel(tok_ref, tok_next_ref, x_hbm, g_ref, o_ref, buf, sems, *, rows):
    slot = _prefetched_rows(tok_ref, tok_next_ref, x_hbm, buf, sems, rows)
    o_ref[...] = _rms_rows(buf[slot], g_ref[...]).astype(o_ref.dtype)


def moe_gather_norm(x, norm_g, tok_buf, tm):
    p = tok_buf.shape[0]
    nb = p // tm
    tok = tok_buf.reshape(nb, 1, tm)
    return pl.pallas_call(
        functools.partial(_gather_norm_kernel, rows=tm),
        grid=(nb,),
        in_specs=_index_specs(tm, nb) + [pl.BlockSpec(memory_space=pl.ANY),
                                         pl.BlockSpec((1, D_MODEL), lambda i: (0, 0))],
        out_specs=pl.BlockSpec((tm, D_MODEL), lambda i: (i, 0)),
        scratch_shapes=[pltpu.VMEM((2, tm, D_MODEL), F32), pltpu.SemaphoreType.DMA((2,))],
        out_shape=jax.ShapeDtypeStruct((p, D_MODEL), BF16),
        compiler_params=_cparams(("arbitrary",)),
        name="moe_gather_norm",
    )(tok, tok, x, norm_g.reshape(1, D_MODEL))


def _expert_weights(be_ref, nxt_ref, w_hbms, stage, w_bf, sems, *, layer, tn):
    j = pl.program_id(0)
    i = pl.program_id(1)
    e = be_ref[i]

    def copies(expert, jj):
        col = pl.multiple_of(jj * tn, tn)
        return [pltpu.make_async_copy(w.at[layer, expert, :, pl.ds(col, tn)], stage.at[k], sems.at[k])
                for k, w in enumerate(w_hbms)]

    def start(expert, jj):
        for cp in copies(expert, jj):
            cp.start()

    @pl.when((i == 0) | (be_ref[jnp.maximum(i - 1, 0)] != e))
    def _():
        @pl.when((i == 0) & (j == 0))
        def _():
            start(e, j)

        for cp in copies(e, j):
            cp.wait()
        for k in range(len(w_hbms)):
            w_bf[k] = stage[k].astype(BF16)
        nxt = nxt_ref[i]

        @pl.when(nxt >= 0)
        def _():
            start(nxt, j)

        @pl.when((nxt < 0) & (j + 1 < pl.num_programs(0)))
        def _():
            start(be_ref[0], j + 1)


def _moe_up_kernel(be_ref, nxt_ref, nu_ref, xs_ref, wg_hbm, wu_hbm, h_ref, stage, w_bf, sems, *, layer, tn):
    i = pl.program_id(1)

    @pl.when(i < nu_ref[0])
    def _():
        _expert_weights(be_ref, nxt_ref, (wg_hbm, wu_hbm), stage, w_bf, sems, layer=layer, tn=tn)
        x = xs_ref[...]
        g = _dot(x, w_bf[0])
        u = _dot(x, w_bf[1])
        h_ref[...] = (jax.nn.silu(g) * u).astype(h_ref.dtype)

    @pl.when(i >= nu_ref[0])
    def _():
        h_ref[...] = jnp.zeros_like(h_ref)


def moe_up(xs, w_gate, w_up, layer, block_e, next_e, n_used, tm, tn=512):
    p = xs.shape[0]
    nb = p // tm
    clamp = lambda i, nu: jnp.minimum(i, nu[0] - 1)
    return pl.pallas_call(
        functools.partial(_moe_up_kernel, layer=layer, tn=tn),
        grid_spec=pltpu.PrefetchScalarGridSpec(
            num_scalar_prefetch=3,
            grid=(EXPERT_FF // tn, nb),
            in_specs=[pl.BlockSpec((tm, D_MODEL), lambda j, i, be, nx, nu: (clamp(i, nu), 0)),
                      pl.BlockSpec(memory_space=pl.ANY), pl.BlockSpec(memory_space=pl.ANY)],
            out_specs=pl.BlockSpec((tm, tn), lambda j, i, be, nx, nu: (i, j)),
            scratch_shapes=[pltpu.VMEM((2, D_MODEL, tn), F32), pltpu.VMEM((2, D_MODEL, tn), BF16),
                            pltpu.SemaphoreType.DMA((2,))]),
        out_shape=jax.ShapeDtypeStruct((p, EXPERT_FF), BF16),
        compiler_params=_cparams(("arbitrary", "arbitrary")),
        name="moe_up",
    )(block_e, next_e, n_used, xs, w_gate, w_up)


def _moe_down_kernel(be_ref, nxt_ref, nu_ref, h_ref, wd_hbm, y_ref, stage, w_bf, sems, *, layer, tn):
    i = pl.program_id(1)

    @pl.when(i < nu_ref[0])
    def _():
        _expert_weights(be_ref, nxt_ref, (wd_hbm,), stage, w_bf, sems, layer=layer, tn=tn)
        y_ref[...] = _dot(h_ref[...], w_bf[0])

    @pl.when(i >= nu_ref[0])
    def _():
        y_ref[...] = jnp.zeros_like(y_ref)


def moe_down(h, w_down, layer, block_e, next_e, n_used, tm, tn=2048):
    p = h.shape[0]
    nb = p // tm
    clamp = lambda i, nu: jnp.minimum(i, nu[0] - 1)
    return pl.pallas_call(
        functools.partial(_moe_down_kernel, layer=layer, tn=tn),
        grid_spec=pltpu.PrefetchScalarGridSpec(
            num_scalar_prefetch=3,
            grid=(D_MODEL // tn, nb),
            in_specs=[pl.BlockSpec((tm, EXPERT_FF), lambda j, i, be, nx, nu: (clamp(i, nu), 0)),
                      pl.BlockSpec(memory_space=pl.ANY)],
            out_specs=pl.BlockSpec((tm, tn), lambda j, i, be, nx, nu: (i, j)),
            scratch_shapes=[pltpu.VMEM((1, EXPERT_FF, tn), F32), pltpu.VMEM((1, EXPERT_FF, tn), BF16),
                            pltpu.SemaphoreType.DMA((1,))]),
        out_shape=jax.ShapeDtypeStruct((p, D_MODEL), F32),
        compiler_params=_cparams(("arbitrary", "arbitrary")),
        name="moe_down",
    )(block_e, next_e, n_used, h, w_down)


def _combine_kernel(pos_ref, pos_next_ref, x_ref, gate_ref, ys_hbm, *rest, tc, split_block):
    out_refs, (buf, sems) = rest[:-2], rest[-2:]
    slot = _prefetched_rows(pos_ref, pos_next_ref, ys_hbm, buf, sems, TOP_K * tc)
    gate = gate_ref[...]
    out = (x_ref[...] + gate[:, 0:1] * buf[slot, pl.ds(0, tc)] + gate[:, 1:2] * buf[slot, pl.ds(tc, tc)])
    if split_block is None:
        out_refs[0][...] = out
    else:
        i = pl.program_id(0)

        @pl.when(i < split_block)
        def _():
            out_refs[0][...] = out

        @pl.when(i >= split_block)
        def _():
            out_refs[1][...] = out


def moe_combine(x, ys, pos, gate, split_rows=None, tc=128):
    n = x.shape[0]
    tc = _tile(n if split_rows is None else math.gcd(split_rows, n - split_rows), tc)
    nb = n // tc
    pos_blk = jnp.swapaxes(pos.reshape(nb, tc, TOP_K), 1, 2).reshape(nb, 1, TOP_K * tc)
    blk = pl.BlockSpec((tc, D_MODEL), lambda i: (i, 0))
    if split_rows is None:
        split_block = None
        out_specs = blk
        out_shape = jax.ShapeDtypeStruct((n, D_MODEL), F32)
    else:
        split_block = split_rows // tc
        out_specs = [pl.BlockSpec((tc, D_MODEL), lambda i: (jnp.minimum(i, split_block - 1), 0)),
                     pl.BlockSpec((tc, D_MODEL), lambda i: (jnp.maximum(i - split_block, 0), 0))]
        out_shape = [jax.ShapeDtypeStruct((split_rows, D_MODEL), F32),
                     jax.ShapeDtypeStruct((n - split_rows, D_MODEL), F32)]
    return pl.pallas_call(
        functools.partial(_combine_kernel, tc=tc, split_block=split_block),
        grid=(nb,),
        in_specs=_index_specs(TOP_K * tc, nb) + [blk, pl.BlockSpec((tc, TOP_K), lambda i: (i, 0)),
                                                 pl.BlockSpec(memory_space=pl.ANY)],
        out_specs=out_specs,
        out_shape=out_shape,
        scratch_shapes=[pltpu.VMEM((2, TOP_K * tc, D_MODEL), F32), pltpu.SemaphoreType.DMA((2,))],
        compiler_params=_cparams(("arbitrary",)),
        name="moe_combine",
    )(pos_blk, pos_blk, x, gate, ys)


def hier_moe_residual(x, norm_g, layer, w_group, b_group, w_router, b_router, w_gate, w_up, w_down,
                      split_rows=None, tm=MOE_BLOCK):
    eid, gate = moe_route(x, norm_g, w_group, b_group, w_router, b_router)
    tok_buf, block_e, next_e, n_used, pos = _dispatch_plan(eid, tm)
    xs = moe_gather_norm(x, norm_g, tok_buf, tm)
    h = moe_up(xs, w_gate, w_up, layer, block_e, next_e, n_used, tm)
    ys = moe_down(h, w_down, layer, block_e, next_e, n_used, tm)
    return moe_combine(x, ys, pos, gate, split_rows)


def encoder_layer(x, layer, segs, lb_f, lb_b, norm_mix, w_in, gmlp_v_norm, gmlp_w_s, gmlp_b_s,
                  hgrn_out_norm, conv_w, mla_q_norm, mla_kv_norm, mla_w_uq, mla_w_ukv, qk_norm_q,
                  qk_norm_k, w_branch, w_out, norm_ffn, w_group, b_group, w_router, b_router,
                  w_gate, w_up, w_down, split_rows=None):
    w_in_l = w_in[layer]
    w_abc = w_in_l[:, :ABC_WIDTH].astype(BF16)
    w_qkv = jnp.pad(w_in_l[:, ABC_WIDTH:ABC_WIDTH + QKV_WIDTH], ((0, 0), (0, QKV_PAD - QKV_WIDTH))).astype(BF16)
    w_gl = w_in_l[:, ABC_WIDTH + QKV_WIDTH:].astype(BF16)

    xn = rmsnorm_rows(x, norm_mix[layer])
    pabc = matmul(xn, w_abc, out_dtype=F32, tm=1024, tn=1024, name="proj_abc")
    pqkv = matmul(xn, w_qkv, out_dtype=F32, tm=1024, tn=384, name="proj_qkv")
    gates = matmul(xn, w_gl, out_dtype=BF16, tm=1024, tn=1024, act="sigmoid", name="proj_gates")

    branches = (
        gmlp_mixer(pabc, gmlp_v_norm[layer], gmlp_w_s[layer], gmlp_b_s[layer]),
        hgrn_mixer(pabc, lb_f, lb_b, hgrn_out_norm[layer], segs),
        conv_mixer(pabc, conv_w[layer], segs),
        mla_mixer(pqkv, mla_q_norm[layer], mla_kv_norm[layer], mla_w_uq[layer], mla_w_ukv[layer],
                  qk_norm_q[layer], qk_norm_k[layer], segs),
    )
    merged = merge_branches(branches, w_branch[layer].astype(BF16), gates)
    x = matmul(merged, w_out[layer].astype(BF16), out_dtype=F32, tm=1024, tn=1024, res=x, name="out_proj")
    return hier_moe_residual(x, norm_ffn[layer], layer, w_group[layer], b_group[layer], w_router[layer],
                             b_router[layer], w_gate, w_up, w_down, split_rows)


def kernel(x_prompt, x_sample, norm_mix, w_in, gmlp_v_norm, gmlp_w_s, gmlp_b_s, hgrn_lb_logits, hgrn_out_norm, conv_w, mla_q_norm, mla_kv_norm, mla_w_uq, mla_w_ukv, qk_norm_q, qk_norm_k, w_branch, w_out, norm_ffn, w_group, b_group, w_router, b_router, w_gate, w_up, w_down):
    depth = norm_mix.shape[0]
    bp, sp, d = x_prompt.shape
    bs, ss, _ = x_sample.shape
    n_p, n_s = bp * sp, bs * ss
    segs = ((0, n_p, sp), (n_p, n_s, ss))
    x = jnp.concatenate([x_prompt.reshape(n_p, d), x_sample.reshape(n_s, d)], axis=0)

    lb_p = jax.nn.softmax(hgrn_lb_logits.astype(F32), axis=1)
    lb = jnp.maximum(jnp.cumsum(lb_p, axis=1) - lb_p[:, :1], 0.0)

    for layer in range(depth):
        x = encoder_layer(x, layer, segs, lb[0, layer], lb[1, layer], norm_mix, w_in, gmlp_v_norm, gmlp_w_s,
                          gmlp_b_s, hgrn_out_norm, conv_w, mla_q_norm, mla_kv_norm, mla_w_uq, mla_w_ukv,
                          qk_norm_q, qk_norm_k, w_branch, w_out, norm_ffn, w_group, b_group, w_router,
                          b_router, w_gate, w_up, w_down, split_rows=n_p if layer == depth - 1 else None)
    y_p, y_s = x
    return (y_p.reshape(bp, sp, d), y_s.reshape(bs, ss, d))
```

```python
import functools
import math

import jax
import jax.numpy as jnp
import numpy as np
from jax import lax
from jax.experimental import pallas as pl
from jax.experimental.pallas import tpu as pltpu

F32 = jnp.float32
BF16 = jnp.bfloat16

D_MODEL = 4096
BRANCH_WIDTH = 1024
HEAD_DIM = 128
N_HEADS = 8
GMLP_CHUNK = 128
MLA_Q_RANK = 768
MLA_KV_RANK = 256
MLA_NOPE = 128
MLA_ROPE = 64
MLA_QK = MLA_NOPE + MLA_ROPE
MLA_QK_PAD = 256
ROPE_BASE = 10000.0
N_GROUPS = 8
EXPERTS_PER_GROUP = 8
N_EXPERTS = 64
TOP_K = 2
EXPERT_FF = 1024
EPS = 1e-6
ABC_WIDTH = 10 * BRANCH_WIDTH
QKV_WIDTH = MLA_Q_RANK + MLA_KV_RANK + MLA_ROPE
QKV_PAD = 1152
LANES = 128
VMEM_LIMIT = 56 * 1024 * 1024
HGRN_CHUNK = 64
MOE_BLOCK = 256
ROWS_PER_ISSUE_STEP = 8
NEG_INF = float("-inf")


def _cparams(sem):
    return pltpu.CompilerParams(dimension_semantics=sem, vmem_limit_bytes=VMEM_LIMIT)


def _tile(n, t):
    t = min(t, n)
    while n % t:
        t //= 2
    return t


def _dot(a, b):
    return jnp.dot(a, b, preferred_element_type=F32)


def _dot_nt(a, b):
    return lax.dot_general(a, b, (((1,), (1,)), ((), ())), preferred_element_type=F32)


def _rms_rows(x, g):
    return x * lax.rsqrt(jnp.mean(x * x, axis=-1, keepdims=True) + EPS) * g


def _as_parts(x):
    return tuple(x) if isinstance(x, (tuple, list)) else (x,)


def _part_tile(parts, tm):
    return _tile(functools.reduce(math.gcd, [p.shape[0] for p in parts]), tm)


def _part_ranges(parts, tm):
    ranges, off = [], 0
    for p in parts:
        ranges.append((off, p.shape[0] // tm))
        off += p.shape[0] // tm
    return ranges


def _when_in_part(i, off, nblk):
    return pl.when((i >= off) & (i < off + nblk))


def _rmsnorm_kernel(*refs, ranges):
    x_refs, g_ref, o_ref = refs[:-2], refs[-2], refs[-1]
    i = pl.program_id(0)
    for x_ref, (off, nblk) in zip(x_refs, ranges):
        @_when_in_part(i, off, nblk)
        def _(x_ref=x_ref):
            o_ref[...] = _rms_rows(x_ref[...], g_ref[...]).astype(o_ref.dtype)


def rmsnorm_rows(x, g, tm=512):
    parts = _as_parts(x)
    d = parts[0].shape[1]
    tm = _part_tile(parts, tm)
    ranges = _part_ranges(parts, tm)
    n_blocks = sum(nblk for _, nblk in ranges)
    x_specs = [pl.BlockSpec((tm, d), functools.partial(lambda i, off, nblk: (jnp.clip(i - off, 0, nblk - 1), 0),
                                                       off=off, nblk=nblk)) for off, nblk in ranges]
    return pl.pallas_call(
        functools.partial(_rmsnorm_kernel, ranges=ranges),
        grid=(n_blocks,),
        in_specs=x_specs + [pl.BlockSpec((1, d), lambda i: (0, 0))],
        out_specs=pl.BlockSpec((tm, d), lambda i: (i, 0)),
        out_shape=jax.ShapeDtypeStruct((n_blocks * tm, d), BF16),
        compiler_params=_cparams(("parallel",)),
        name="rmsnorm",
    )(*parts, g.reshape(1, d))


def _mm_kernel(x_ref, w_ref, *rest, act, res_ranges):
    res_refs, o_ref = rest[:-1], rest[-1]
    acc = _dot(x_ref[...], w_ref[...])
    if act == "sigmoid":
        acc = jax.nn.sigmoid(acc)
    if not res_refs:
        o_ref[...] = acc.astype(o_ref.dtype)
    i = pl.program_id(0)
    for r_ref, (off, nblk) in zip(res_refs, res_ranges):
        @_when_in_part(i, off, nblk)
        def _(r_ref=r_ref):
            o_ref[...] = (acc + r_ref[...]).astype(o_ref.dtype)


def matmul(x, w, *, out_dtype, tm, tn, act=None, res=None, name="matmul"):
    m, k = x.shape
    n = w.shape[1]
    res_parts = () if res is None else _as_parts(res)
    tm = _part_tile(res_parts + (x,), tm)
    nj = n // tn
    res_ranges = _part_ranges(res_parts, tm)

    def res_index(i, j, off, nblk):
        return (jnp.clip(i - off, 0, nblk - 1), jnp.where(i < off, 0, jnp.where(i >= off + nblk, nj - 1, j)))

    in_specs = [pl.BlockSpec((tm, k), lambda i, j: (i, 0)),
                pl.BlockSpec((k, tn), lambda i, j: (0, j))]
    in_specs += [pl.BlockSpec((tm, tn), functools.partial(res_index, off=off, nblk=nblk))
                 for off, nblk in res_ranges]
    return pl.pallas_call(
        functools.partial(_mm_kernel, act=act, res_ranges=res_ranges),
        grid=(m // tm, nj),
        in_specs=in_specs,
        out_specs=pl.BlockSpec((tm, tn), lambda i, j: (i, j)),
        out_shape=jax.ShapeDtypeStruct((m, n), out_dtype),
        compiler_params=_cparams(("parallel", "parallel")),
        name=name,
    )(x, w, *res_parts)


def _merge_kernel(b0, b1, b2, b3, wb_ref, g0, g1, g2, g3, o_ref):
    acc = None
    for kb, (b_ref, g_ref) in enumerate(((b0, g0), (b1, g1), (b2, g2), (b3, g3))):
        t = _dot(b_ref[...], wb_ref[kb]) * g_ref[...].astype(F32)
        acc = t if acc is None else acc + t
    o_ref[...] = acc.astype(o_ref.dtype)


def merge_branches(branches, w_branch, gates, tm=1024, tn=512):
    n = branches[0].shape[0]
    tm = _tile(n, tm)
    nj = D_MODEL // tn
    b_spec = pl.BlockSpec((tm, BRANCH_WIDTH), lambda i, j: (i, 0))
    g_specs = [pl.BlockSpec((tm, tn), functools.partial(lambda i, j, kb: (i, kb * nj + j), kb=kb))
               for kb in range(4)]
    return pl.pallas_call(
        _merge_kernel,
        grid=(n // tm, nj),
        in_specs=[b_spec] * 4 + [pl.BlockSpec((4, BRANCH_WIDTH, tn), lambda i, j: (0, 0, j))] + g_specs,
        out_specs=pl.BlockSpec((tm, tn), lambda i, j: (i, j)),
        out_shape=jax.ShapeDtypeStruct((n, D_MODEL), BF16),
        compiler_params=_cparams(("parallel", "parallel")),
        name="merge_branches",
    )(*branches, w_branch, gates, gates, gates, gates)


def _gmlp_kernel(u_ref, v_ref, g_ref, ws_ref, bs_ref, o_ref, *, tm):
    for c in range(tm // GMLP_CHUNK):
        rows = pl.ds(c * GMLP_CHUNK, GMLP_CHUNK)
        u = jax.nn.gelu(u_ref[rows, :])
        v = jax.nn.gelu(v_ref[rows, :])
        ms = jnp.mean(v * v, axis=-1, keepdims=True)
        vb = (v * lax.rsqrt(ms + EPS) * g_ref[...]).astype(BF16)
        for h in range(N_HEADS):
            cols = slice(h * HEAD_DIM, (h + 1) * HEAD_DIM)
            mixed = _dot(ws_ref[h], vb[:, cols]) + bs_ref[:, cols]
            o_ref[rows, cols] = (u[:, cols] * mixed).astype(o_ref.dtype)


def gmlp_mixer(pabc, v_gain, w_s, b_s, tm=512):
    n = pabc.shape[0]
    tm = _tile(n, tm)
    bias = jnp.repeat(jnp.swapaxes(b_s, 0, 1), HEAD_DIM, axis=1)
    return pl.pallas_call(
        functools.partial(_gmlp_kernel, tm=tm),
        grid=(n // tm,),
        in_specs=[pl.BlockSpec((tm, BRANCH_WIDTH), lambda i: (i, 0)),
                  pl.BlockSpec((tm, BRANCH_WIDTH), lambda i: (i, 1)),
                  pl.BlockSpec((1, BRANCH_WIDTH), lambda i: (0, 0)),
                  pl.BlockSpec((N_HEADS, GMLP_CHUNK, GMLP_CHUNK), lambda i: (0, 0, 0)),
                  pl.BlockSpec((GMLP_CHUNK, BRANCH_WIDTH), lambda i: (0, 0))],
        out_specs=pl.BlockSpec((tm, BRANCH_WIDTH), lambda i: (i, 0)),
        out_shape=jax.ShapeDtypeStruct((n, BRANCH_WIDTH), BF16),
        compiler_params=_cparams(("parallel",)),
        name="gmlp_mixer",
    )(pabc, pabc, v_gain.reshape(1, -1), w_s.astype(BF16), bias)


def _segment_of(row0, segs):
    seq_len = jnp.int32(segs[-1][2])
    first = jnp.int32(segs[-1][0])
    for (f, nr, s) in reversed(segs[:-1]):
        inside = row0 < f + nr
        seq_len = jnp.where(inside, s, seq_len)
        first = jnp.where(inside, f, first)
    return seq_len, first


def _conv_kernel(bg_ref, cg_ref, h_ref, cgp_ref, hp_ref, cgn_ref, hn_ref, w_ref, o_ref, *, tm, segs):
    row0 = pl.program_id(0) * tm
    seq_len, first = _segment_of(row0, segs)
    at_start = lax.rem(row0 - first, seq_len) == 0
    at_end = lax.rem(row0 - first + tm, seq_len) == 0
    z = cg_ref[...] * h_ref[...]
    left = jnp.where(at_start, 0.0, cgp_ref[7:8, :] * hp_ref[7:8, :])
    right = jnp.where(at_end, 0.0, cgn_ref[0:1, :] * hn_ref[0:1, :])
    ridx = lax.broadcasted_iota(jnp.int32, z.shape, 0)
    z_prev = jnp.where(ridx == 0, left, pltpu.roll(z, 1, axis=0))
    z_next = jnp.where(ridx == tm - 1, right, pltpu.roll(z, tm - 1, axis=0))
    y = w_ref[0:1, :] * z_prev + w_ref[1:2, :] * z + w_ref[2:3, :] * z_next
    o_ref[...] = (bg_ref[...] * y).astype(o_ref.dtype)


def conv_mixer(pabc, conv_w, segs, tm=256):
    n = pabc.shape[0]
    tm = _tile(n, tm)
    nb8 = n // 8
    blk = lambda c: pl.BlockSpec((tm, BRANCH_WIDTH), lambda i: (i, c))
    prev8 = lambda c: pl.BlockSpec((8, BRANCH_WIDTH), lambda i: (jnp.maximum(i * (tm // 8) - 1, 0), c))
    next8 = lambda c: pl.BlockSpec((8, BRANCH_WIDTH), lambda i: (jnp.minimum((i + 1) * (tm // 8), nb8 - 1), c))
    return pl.pallas_call(
        functools.partial(_conv_kernel, tm=tm, segs=segs),
        grid=(n // tm,),
        in_specs=[blk(7), blk(8), blk(9), prev8(8), prev8(9), next8(8), next8(9),
                  pl.BlockSpec((3, BRANCH_WIDTH), lambda i: (0, 0))],
        out_specs=pl.BlockSpec((tm, BRANCH_WIDTH), lambda i: (i, 0)),
        out_shape=jax.ShapeDtypeStruct((n, BRANCH_WIDTH), BF16),
        compiler_params=_cparams(("parallel",)),
        name="conv_mixer",
    )(pabc, pabc, pabc, pabc, pabc, pabc, pabc, conv_w)


def _hgrn_constants(c, reverse):
    n_lev = int(math.log2(c))
    t = np.arange(c)[:, None]
    u = np.arange(c)[None, :]
    if not reverse:
        w_a = u <= t
        w_kd = u > t
    else:
        w_a = u >= t
        w_kd = u < t
    ws = [w_a, w_kd]
    masks = [np.eye(c, dtype=bool)]
    for lev in range(n_lev):
        m = c >> (lev + 1)
        blk_t = t // (2 * m)
        upper_t = (t % (2 * m)) >= m
        p_last_lower = blk_t * 2 * m + m - 1
        if not reverse:
            w_q = (u > p_last_lower) & (u <= t)
            w_k = (u > t) & (u <= p_last_lower)
            w = np.where(upper_t, w_q, w_k)
            query_t = upper_t
        else:
            p_first_upper = p_last_lower + 1
            w_q = (u >= t) & (u < p_first_upper)
            w_k = (u >= p_first_upper) & (u < t)
            w = np.where(upper_t, w_k, w_q)
            query_t = ~upper_t
        ws.append(w)
        same_blk = blk_t == blk_t.T
        masks.append(same_blk & query_t & (~query_t).T)
    return np.stack(ws).astype(np.float32), np.stack(masks).astype(np.float32)


def _hgrn_pair_constants(c):
    ws_f, masks_f = _hgrn_constants(c, False)
    ws_b, masks_b = _hgrn_constants(c, True)

    def blockdiag(a, b):
        out = np.zeros((a.shape[0], 2 * c, 2 * c), np.float32)
        out[:, :c, :c] = a
        out[:, c:, c:] = b
        return out

    w_all = blockdiag(ws_f, ws_b).reshape(-1, 2 * c)
    return jnp.asarray(w_all, BF16), jnp.asarray(blockdiag(masks_f, masks_b), F32)


def _log_forget_and_key(z, log_lb, log1m_lb):
    log_sig = jnp.minimum(z, 0.0) - jnp.log1p(jnp.exp(-jnp.abs(z)))
    x2 = log1m_lb + log_sig
    hi = jnp.maximum(log_lb, x2)
    lf = hi + jnp.log1p(jnp.exp(-jnp.abs(log_lb - x2)))
    return lf, jnp.exp(x2 - z)


def _split3(x):
    p1 = x.astype(BF16)
    r1 = x - p1.astype(F32)
    p2 = r1.astype(BF16)
    p3 = (r1 - p2.astype(F32)).astype(BF16)
    return p1, p2, p3


def _hgrn_kernel(qf_ref, vf_ref, zf_ref, qb_ref, vb_ref, zb_ref, lb_ref, wall_ref, mask_ref,
                 of_ref, ob_ref, st_ref, *, c, n_lev, segs):
    step = pl.program_id(0)
    row_f = step * c
    row_b = (pl.num_programs(0) - 1 - step) * c
    len_f, first_f = _segment_of(row_f, segs)
    len_b, first_b = _segment_of(row_b, segs)

    @pl.when(lax.rem(row_f - first_f, len_f) == 0)
    def _():
        st_ref[0] = jnp.zeros(st_ref.shape[1:], F32)

    @pl.when(lax.rem(row_b - first_b + c, len_b) == 0)
    def _():
        st_ref[1] = jnp.zeros(st_ref.shape[1:], F32)

    w_all = wall_ref[...]
    q = jnp.concatenate([qf_ref[...], qb_ref[...]], axis=0)
    v = jnp.concatenate([vf_ref[...], vb_ref[...]], axis=0)
    z = jnp.concatenate([zf_ref[...], zb_ref[...]], axis=0)
    row = lax.broadcasted_iota(jnp.int32, q.shape, 0)
    is_fwd = row < c
    lf, k = _log_forget_and_key(z, jnp.where(is_fwd, lb_ref[0:1, :], lb_ref[2:3, :]),
                                jnp.where(is_fwd, lb_ref[1:2, :], lb_ref[3:4, :]))
    p1, p2, p3 = _split3(lf)
    e_all = _dot(w_all, p1) + _dot(w_all, p2) + _dot(w_all, p3)
    c2 = 2 * c
    a = e_all[0:c2]
    qd = (q * jnp.exp(a)).astype(BF16)
    kd = (k * jnp.exp(e_all[c2:2 * c2])).astype(BF16)
    q16 = q.astype(BF16)
    k16 = k.astype(BF16)
    v16 = v.astype(BF16)
    xms = []
    for lev in range(n_lev):
        upper = ((row >> (n_lev - 1 - lev)) & 1) == 1
        is_query = upper == is_fwd
        dm = jnp.exp(e_all[(2 + lev) * c2:(3 + lev) * c2])
        xms.append((jnp.where(is_query, q, k) * dm).astype(BF16))
    decay_f = jnp.exp(a[c - 1:c])
    decay_b = jnp.exp(a[c:c + 1])
    for h in range(N_HEADS):
        cols = slice(h * HEAD_DIM, (h + 1) * HEAD_DIM)
        scores = mask_ref[0] * _dot_nt(q16[:, cols], k16[:, cols])
        for lev in range(n_lev):
            xm = xms[lev][:, cols]
            scores = scores + mask_ref[lev + 1] * _dot_nt(xm, xm)
        o_intra = _dot(scores.astype(BF16), v16[:, cols])
        st_f = st_ref[0, h]
        st_b = st_ref[1, h]
        of_ref[:, cols] = o_intra[:c] + _dot_nt(qd[:c, cols], st_f.astype(BF16))
        ob_ref[:, cols] = o_intra[c:] + _dot_nt(qd[c:, cols], st_b.astype(BF16))
        st_ref[0, h] = st_f * decay_f[:, cols] + _dot(v[:c, cols].T.astype(BF16), kd[:c, cols])
        st_ref[1, h] = st_b * decay_b[:, cols] + _dot(v[c:, cols].T.astype(BF16), kd[c:, cols])


def _hgrn_out_kernel(of_ref, ob_ref, g_ref, gain_ref, o_ref):
    for h in range(N_HEADS):
        cols = slice(h * HEAD_DIM, (h + 1) * HEAD_DIM)
        o = of_ref[:, cols] + ob_ref[:, cols]
        ms = jnp.mean(o * o, axis=-1, keepdims=True)
        o_ref[:, cols] = (o * lax.rsqrt(ms + EPS) * gain_ref[...] * jax.nn.silu(g_ref[:, cols])).astype(o_ref.dtype)


def hgrn_mixer(pabc, lb_f, lb_b, out_gain, segs, c=HGRN_CHUNK, tm=512):
    n = pabc.shape[0]
    n_lev = int(math.log2(c))
    n_chunks = n // c
    w_all, masks = _hgrn_pair_constants(c)
    log_lb = jnp.stack([jnp.log(lb_f), jnp.log1p(-lb_f), jnp.log(lb_b), jnp.log1p(-lb_b)])
    fwd = lambda col: pl.BlockSpec((c, BRANCH_WIDTH), lambda i: (i, col))
    bwd = lambda col: pl.BlockSpec((c, BRANCH_WIDTH), lambda i: (n_chunks - 1 - i, col))
    o_f, o_b = pl.pallas_call(
        functools.partial(_hgrn_kernel, c=c, n_lev=n_lev, segs=segs),
        grid=(n_chunks,),
        in_specs=[fwd(2), fwd(3), fwd(4), bwd(2), bwd(3), bwd(5),
                  pl.BlockSpec((4, BRANCH_WIDTH), lambda i: (0, 0)),
                  pl.BlockSpec(w_all.shape, lambda i: (0, 0)),
                  pl.BlockSpec(masks.shape, lambda i: (0, 0, 0))],
        out_specs=[fwd(0), bwd(0)],
        out_shape=[jax.ShapeDtypeStruct((n, BRANCH_WIDTH), F32)] * 2,
        scratch_shapes=[pltpu.VMEM((2, N_HEADS, HEAD_DIM, HEAD_DIM), F32)],
        compiler_params=_cparams(("arbitrary",)),
        name="hgrn_scan",
    )(pabc, pabc, pabc, pabc, pabc, pabc, log_lb, w_all, masks)
    tm = _tile(n, tm)
    blk = lambda col: pl.BlockSpec((tm, BRANCH_WIDTH), lambda i: (i, col))
    return pl.pallas_call(
        _hgrn_out_kernel,
        grid=(n // tm,),
        in_specs=[blk(0), blk(0), blk(6), pl.BlockSpec((1, HEAD_DIM), lambda i: (0, 0))],
        out_specs=blk(0),
        out_shape=jax.ShapeDtypeStruct((n, BRANCH_WIDTH), BF16),
        compiler_params=_cparams(("parallel",)),
        name="hgrn_out",
    )(o_f, o_b, pabc, out_gain.reshape(1, HEAD_DIM))


def _rope_tile(r, cos_t, sin_a, sin_b):
    return r * cos_t + pltpu.roll(r, 96, axis=1) * sin_a + pltpu.roll(r, 32, axis=1) * sin_b


def _mla_prep_kernel(cq_ref, ckv_ref, wq_ref, wkv_ref, qg_ref, kvg_ref, gq_ref, gk_ref,
                     cos_ref, sina_ref, sinb_ref, q_out, k_out, v_out):
    cq = cq_ref[...]
    hq = (cq * lax.rsqrt(jnp.mean(cq * cq, axis=-1, keepdims=True) + EPS) * qg_ref[...]).astype(BF16)
    qf = _dot(hq, wq_ref[...])
    ckvkr = ckv_ref[...]
    ckv = ckvkr[:, :MLA_KV_RANK]
    kr = ckvkr[:, MLA_KV_RANK:]
    hk = (ckv * lax.rsqrt(jnp.mean(ckv * ckv, axis=-1, keepdims=True) + EPS) * kvg_ref[...]).astype(BF16)
    kvf = _dot(hk, wkv_ref[...])
    cos_t, sin_a, sin_b = cos_ref[...], sina_ref[...], sinb_ref[...]
    gq_n, gq_r = gq_ref[:, :MLA_NOPE], gq_ref[:, MLA_NOPE:]
    gk_n, gk_r = gk_ref[:, :MLA_NOPE], gk_ref[:, MLA_NOPE:]
    kr_ss = jnp.sum(kr * kr, axis=-1, keepdims=True)
    kr_rot = _rope_tile(kr * gk_r, cos_t, sin_a, sin_b)
    scale = MLA_QK ** -0.5 * math.log2(math.e)
    lane = lax.broadcasted_iota(jnp.int32, (cq.shape[0], HEAD_DIM), 1)
    ones_col = jnp.where(lane == 0, 1.0, 0.0).astype(BF16)
    for h in range(N_HEADS):
        base = h * MLA_QK_PAD
        qn = qf[:, base:base + MLA_NOPE]
        qr = qf[:, base + MLA_NOPE:base + MLA_QK_PAD]
        ss = jnp.sum(qn * qn, axis=-1, keepdims=True) + jnp.sum(qr * qr, axis=-1, keepdims=True)
        rstd = lax.rsqrt(ss * (1.0 / MLA_QK) + EPS) * scale
        q_out[h, :, :MLA_NOPE] = (qn * rstd * gq_n).astype(BF16)
        q_out[h, :, MLA_NOPE:] = _rope_tile(qr * rstd * gq_r, cos_t, sin_a, sin_b).astype(BF16)
        kn = kvf[:, base:base + MLA_NOPE]
        ssk = jnp.sum(kn * kn, axis=-1, keepdims=True) + kr_ss
        rstdk = lax.rsqrt(ssk * (1.0 / MLA_QK) + EPS)
        k_out[h, :, :MLA_NOPE] = (kn * rstdk * gk_n).astype(BF16)
        k_out[h, :, MLA_NOPE:] = (kr_rot * rstdk).astype(BF16)
        v_out[h, :, :HEAD_DIM] = kvf[:, base + MLA_NOPE:base + MLA_QK_PAD].astype(BF16)
        v_out[h, :, HEAD_DIM:] = ones_col


def _rope_tables(segs):
    half = MLA_ROPE // 2
    inv_freq = ROPE_BASE ** (-jnp.arange(half, dtype=F32) * 2.0 / MLA_ROPE)
    pos = jnp.concatenate([jnp.tile(jnp.arange(s, dtype=F32), nr // s) for (_, nr, s) in segs])
    ang = pos[:, None] * inv_freq[None, :]
    cos, sin = jnp.cos(ang), jnp.sin(ang)
    zero = jnp.zeros_like(cos)
    zero2 = jnp.concatenate([zero, zero], axis=1)
    cos_t = jnp.concatenate([cos, cos, zero2], axis=1)
    sin_a = jnp.concatenate([-sin, zero, zero2], axis=1)
    sin_b = jnp.concatenate([zero, sin, zero2], axis=1)
    return cos_t, sin_a, sin_b


def mla_prep(pqkv, q_gain, kv_gain, w_uq, w_ukv, gq, gk, segs, tm=256):
    n = pqkv.shape[0]
    tm = _tile(n, tm)
    wq = jnp.pad(w_uq.reshape(MLA_Q_RANK, N_HEADS, MLA_QK),
                 ((0, 0), (0, 0), (0, MLA_QK_PAD - MLA_QK))).reshape(MLA_Q_RANK, -1).astype(BF16)
    wkv = w_ukv.astype(BF16)
    pad = lambda g: jnp.pad(g, (0, MLA_QK_PAD - MLA_QK)).reshape(1, MLA_QK_PAD)
    cos_t, sin_a, sin_b = _rope_tables(segs)
    const = lambda shape: pl.BlockSpec(shape, lambda i: (0, 0))
    tab = pl.BlockSpec((tm, LANES), lambda i: (i, 0))
    hd = lambda w: pl.BlockSpec((N_HEADS, tm, w), lambda i: (0, i, 0))
    return pl.pallas_call(
        _mla_prep_kernel,
        grid=(n // tm,),
        in_specs=[pl.BlockSpec((tm, MLA_Q_RANK), lambda i: (i, 0)),
                  pl.BlockSpec((tm, QKV_PAD - MLA_Q_RANK), lambda i: (i, 2)),
                  const(wq.shape), const(wkv.shape),
                  const((1, MLA_Q_RANK)), const((1, MLA_KV_RANK)),
                  const((1, MLA_QK_PAD)), const((1, MLA_QK_PAD)), tab, tab, tab],
        out_specs=[hd(MLA_QK_PAD), hd(MLA_QK_PAD), hd(2 * HEAD_DIM)],
        out_shape=[jax.ShapeDtypeStruct((N_HEADS, n, MLA_QK_PAD), BF16),
                   jax.ShapeDtypeStruct((N_HEADS, n, MLA_QK_PAD), BF16),
                   jax.ShapeDtypeStruct((N_HEADS, n, 2 * HEAD_DIM), BF16)],
        compiler_params=_cparams(("parallel",)),
        name="mla_prep",
    )(pqkv, pqkv, wq, wkv, q_gain.reshape(1, -1), kv_gain.reshape(1, -1), pad(gq), pad(gk),
      cos_t, sin_a, sin_b)


def _attn_kernel(q_ref, k_ref, v_ref, o_ref, *, kv_chunk):
    q = q_ref[...]
    acc = None
    m_run = None
    for c0 in range(0, k_ref.shape[0], kv_chunk):
        s = _dot_nt(q, k_ref[pl.ds(c0, kv_chunk), :])
        m = jnp.max(s, axis=-1, keepdims=True)
        ol = _dot(jnp.exp2(s - m).astype(BF16), v_ref[pl.ds(c0, kv_chunk), :])
        if acc is None:
            acc, m_run = ol, m
        else:
            m_new = jnp.maximum(m_run, m)
            acc = acc * jnp.exp2(m_run - m_new) + ol * jnp.exp2(m - m_new)
            m_run = m_new
    o_ref[...] = (acc[:, :HEAD_DIM] / acc[:, HEAD_DIM:HEAD_DIM + 1]).astype(o_ref.dtype)


def attention_segment(q, k, v, first_row, n_rows, seq_len, tq=512, kv_chunk=1024):
    tq = min(tq, seq_len)
    kv_chunk = min(kv_chunk, seq_len)
    n_seq = n_rows // seq_len
    qb_per_seq = seq_len // tq
    qoff = first_row // tq
    soff = first_row // seq_len
    return pl.pallas_call(
        functools.partial(_attn_kernel, kv_chunk=kv_chunk),
        grid=(n_seq, N_HEADS, qb_per_seq),
        in_specs=[pl.BlockSpec((None, tq, MLA_QK_PAD), lambda b, h, i: (h, qoff + b * qb_per_seq + i, 0)),
                  pl.BlockSpec((None, seq_len, MLA_QK_PAD), lambda b, h, i: (h, soff + b, 0)),
                  pl.BlockSpec((None, seq_len, 2 * HEAD_DIM), lambda b, h, i: (h, soff + b, 0))],
        out_specs=pl.BlockSpec((tq, HEAD_DIM), lambda b, h, i: (b * qb_per_seq + i, h)),
        out_shape=jax.ShapeDtypeStruct((n_rows, BRANCH_WIDTH), BF16),
        compiler_params=_cparams(("parallel", "parallel", "parallel")),
        name="mla_attention",
    )(q, k, v)


def mla_mixer(pqkv, q_gain, kv_gain, w_uq, w_ukv, gq, gk, segs):
    q, k, v = mla_prep(pqkv, q_gain, kv_gain, w_uq, w_ukv, gq, gk, segs)
    outs = [attention_segment(q, k, v, f, nr, s) for (f, nr, s) in segs]
    return outs[0] if len(outs) == 1 else jnp.concatenate(outs, axis=0)


def _route_kernel(x_ref, g_ref, w_ref, b_ref, o_ref):
    xn = _rms_rows(x_ref[...], g_ref[...]).astype(BF16)
    lg = _dot(xn, w_ref[...]) + b_ref[...]
    lane = lax.broadcasted_iota(jnp.int32, lg.shape, 1)
    big = jnp.int32(1 << 20)
    is_g = (lane >= N_EXPERTS) & (lane < N_EXPERTS + N_GROUPS)
    gl = jnp.where(is_g, lg, NEG_INF)
    gmax = jnp.max(gl, axis=-1, keepdims=True)
    gidx = jnp.min(jnp.where(is_g & (gl == gmax), lane - N_EXPERTS, big), axis=-1, keepdims=True)
    gsum = jnp.sum(jnp.where(is_g, jnp.exp(gl - gmax), 0.0), axis=-1, keepdims=True)
    g_top = 1.0 / gsum
    in_grp = (lane < N_EXPERTS) & ((lane >> 3) == gidx)
    el = jnp.where(in_grp, lg, NEG_INF)
    v1 = jnp.max(el, axis=-1, keepdims=True)
    i1 = jnp.min(jnp.where(in_grp & (el == v1), lane, big), axis=-1, keepdims=True)
    rest = in_grp & (lane != i1)
    el2 = jnp.where(rest, lg, NEG_INF)
    v2 = jnp.max(el2, axis=-1, keepdims=True)
    i2 = jnp.min(jnp.where(rest & (el2 == v2), lane, big), axis=-1, keepdims=True)
    d = jnp.exp(v2 - v1)
    p1 = 1.0 / (1.0 + d)
    p2 = d / (1.0 + d)
    out = jnp.where(lane == 0, i1.astype(F32),
                    jnp.where(lane == 1, i2.astype(F32),
                              jnp.where(lane == 2, g_top * p1,
                                        jnp.where(lane == 3, g_top * p2, 0.0))))
    o_ref[...] = out


def moe_route(x, norm_g, w_group, b_group, w_router, b_router, tm=512):
    n = x.shape[0]
    tm = _tile(n, tm)
    padc = LANES - N_EXPERTS - N_GROUPS
    w = jnp.concatenate([w_router, w_group, jnp.zeros((D_MODEL, padc), F32)], axis=1).astype(BF16)
    b = jnp.concatenate([b_router, b_group, jnp.zeros((padc,), F32)]).reshape(1, LANES)
    r = pl.pallas_call(
        _route_kernel,
        grid=(n // tm,),
        in_specs=[pl.BlockSpec((tm, D_MODEL), lambda i: (i, 0)),
                  pl.BlockSpec((1, D_MODEL), lambda i: (0, 0)),
                  pl.BlockSpec((D_MODEL, LANES), lambda i: (0, 0)),
                  pl.BlockSpec((1, LANES), lambda i: (0, 0))],
        out_specs=pl.BlockSpec((tm, LANES), lambda i: (i, 0)),
        out_shape=jax.ShapeDtypeStruct((n, LANES), F32),
        compiler_params=_cparams(("parallel",)),
        name="moe_route",
    )(x, norm_g.reshape(1, D_MODEL), w, b)
    return r[:, :TOP_K].astype(jnp.int32), r[:, TOP_K:2 * TOP_K]


def _dispatch_plan(eid, tm):
    n = eid.shape[0]
    m = n * TOP_K
    flat_e = eid.reshape(-1)
    onehot = (flat_e[:, None] == jnp.arange(N_EXPERTS, dtype=jnp.int32)[None, :]).astype(jnp.int32)
    csum = jnp.cumsum(onehot, axis=0)
    rank = jnp.take_along_axis(csum, flat_e[:, None], axis=1)[:, 0] - 1
    counts = csum[-1]
    padded = (counts + tm - 1) // tm * tm
    pad_end = jnp.cumsum(padded)
    pad_start = pad_end - padded
    dest = pad_start[flat_e] + rank
    p = (m + tm - 1) // tm * tm + N_EXPERTS * tm
    tok_buf = jnp.zeros((p,), jnp.int32).at[dest].set(jnp.arange(m, dtype=jnp.int32) // TOP_K)
    n_blocks = p // tm
    block_e = jnp.minimum(jnp.searchsorted(pad_end, jnp.arange(n_blocks, dtype=jnp.int32) * tm, side="right"),
                          N_EXPERTS - 1).astype(jnp.int32)
    n_used = (pad_end[-1] // tm).astype(jnp.int32).reshape(1)
    experts = jnp.arange(N_EXPERTS, dtype=jnp.int32)
    first_nonempty_from = lax.cummin(jnp.where(counts > 0, experts, N_EXPERTS)[::-1])[::-1]
    next_nonempty = jnp.concatenate([first_nonempty_from[1:], jnp.full((1,), N_EXPERTS, jnp.int32)])
    next_e = jnp.where(next_nonempty < N_EXPERTS, next_nonempty, -1)[block_e].astype(jnp.int32)
    return tok_buf, block_e, next_e, n_used, dest.reshape(n, TOP_K).astype(jnp.int32)


def _prefetched_rows(idx_ref, idx_next_ref, src_hbm, buf, sems, n_rows):
    i = pl.program_id(0)
    slot = lax.rem(i, 2)

    def start_block(idx, s):
        def body(r4, carry):
            for k in range(ROWS_PER_ISSUE_STEP):
                r = ROWS_PER_ISSUE_STEP * r4 + k
                pltpu.make_async_copy(src_hbm.at[pl.ds(idx[0, 0, r], 1)], buf.at[s, pl.ds(r, 1)],
                                      sems.at[s]).start(priority=k % 2)
            return carry
        lax.fori_loop(0, n_rows // ROWS_PER_ISSUE_STEP, body, 0)

    @pl.when(i == 0)
    def _():
        start_block(idx_ref, slot)

    @pl.when(i + 1 < pl.num_programs(0))
    def _():
        start_block(idx_next_ref, 1 - slot)

    pltpu.make_async_copy(src_hbm.at[pl.ds(0, n_rows)], buf.at[slot], sems.at[slot]).wait()
    return slot


def _index_specs(n_rows, nb):
    cur = pl.BlockSpec((1, 1, n_rows), lambda i: (i, 0, 0), memory_space=pltpu.SMEM)
    nxt = pl.BlockSpec((1, 1, n_rows), lambda i: (jnp.minimum(i + 1, nb - 1), 0, 0), memory_space=pltpu.SMEM)
    return [cur, nxt]


def _gather_norm_kernel(tok_ref, tok_next_ref, x_hbm, g_ref, o_ref, buf, sems, *, rows):
    slot = _prefetched_rows(tok_ref, tok_next_ref, x_hbm, buf, sems, rows)
    o_ref[...] = _rms_rows(buf[slot], g_ref[...]).astype(o_ref.dtype)


def moe_gather_norm(x, norm_g, tok_buf, tm):
    p = tok_buf.shape[0]
    nb = p // tm
    tok = tok_buf.reshape(nb, 1, tm)
    return pl.pallas_call(
        functools.partial(_gather_norm_kernel, rows=tm),
        grid=(nb,),
        in_specs=_index_specs(tm, nb) + [pl.BlockSpec(memory_space=pl.ANY),
                                         pl.BlockSpec((1, D_MODEL), lambda i: (0, 0))],
        out_specs=pl.BlockSpec((tm, D_MODEL), lambda i: (i, 0)),
        scratch_shapes=[pltpu.VMEM((2, tm, D_MODEL), F32), pltpu.SemaphoreType.DMA((2,))],
        out_shape=jax.ShapeDtypeStruct((p, D_MODEL), BF16),
        compiler_params=_cparams(("arbitrary",)),
        name="moe_gather_norm",
    )(tok, tok, x, norm_g.reshape(1, D_MODEL))


def _expert_weights(be_ref, nxt_ref, w_hbms, stage, w_bf, sems, *, layer, tn):
    j = pl.program_id(0)
    i = pl.program_id(1)
    e = be_ref[i]

    def copies(expert, jj):
        col = pl.multiple_of(jj * tn, tn)
        return [pltpu.make_async_copy(w.at[layer, expert, :, pl.ds(col, tn)], stage.at[k], sems.at[k])
                for k, w in enumerate(w_hbms)]

    def start(expert, jj):
        for cp in copies(expert, jj):
            cp.start()

    @pl.when((i == 0) | (be_ref[jnp.maximum(i - 1, 0)] != e))
    def _():
        @pl.when((i == 0) & (j == 0))
        def _():
            start(e, j)

        for cp in copies(e, j):
            cp.wait()
        for k in range(len(w_hbms)):
            w_bf[k] = stage[k].astype(BF16)
        nxt = nxt_ref[i]

        @pl.when(nxt >= 0)
        def _():
            start(nxt, j)

        @pl.when((nxt < 0) & (j + 1 < pl.num_programs(0)))
        def _():
            start(be_ref[0], j + 1)


def _moe_up_kernel(be_ref, nxt_ref, nu_ref, xs_ref, wg_hbm, wu_hbm, h_ref, stage, w_bf, sems, *, layer, tn):
    i = pl.program_id(1)

    @pl.when(i < nu_ref[0])
    def _():
        _expert_weights(be_ref, nxt_ref, (wg_hbm, wu_hbm), stage, w_bf, sems, layer=layer, tn=tn)
        x = xs_ref[...]
        g = _dot(x, w_bf[0])
        u = _dot(x, w_bf[1])
        h_ref[...] = (jax.nn.silu(g) * u).astype(h_ref.dtype)

    @pl.when(i >= nu_ref[0])
    def _():
        h_ref[...] = jnp.zeros_like(h_ref)


def moe_up(xs, w_gate, w_up, layer, block_e, next_e, n_used, tm, tn=512):
    p = xs.shape[0]
    nb = p // tm
    clamp = lambda i, nu: jnp.minimum(i, nu[0] - 1)
    return pl.pallas_call(
        functools.partial(_moe_up_kernel, layer=layer, tn=tn),
        grid_spec=pltpu.PrefetchScalarGridSpec(
            num_scalar_prefetch=3,
            grid=(EXPERT_FF // tn, nb),
            in_specs=[pl.BlockSpec((tm, D_MODEL), lambda j, i, be, nx, nu: (clamp(i, nu), 0)),
                      pl.BlockSpec(memory_space=pl.ANY), pl.BlockSpec(memory_space=pl.ANY)],
            out_specs=pl.BlockSpec((tm, tn), lambda j, i, be, nx, nu: (i, j)),
            scratch_shapes=[pltpu.VMEM((2, D_MODEL, tn), F32), pltpu.VMEM((2, D_MODEL, tn), BF16),
                            pltpu.SemaphoreType.DMA((2,))]),
        out_shape=jax.ShapeDtypeStruct((p, EXPERT_FF), BF16),
        compiler_params=_cparams(("arbitrary", "arbitrary")),
        name="moe_up",
    )(block_e, next_e, n_used, xs, w_gate, w_up)


def _moe_down_kernel(be_ref, nxt_ref, nu_ref, h_ref, wd_hbm, y_ref, stage, w_bf, sems, *, layer, tn):
    i = pl.program_id(1)

    @pl.when(i < nu_ref[0])
    def _():
        _expert_weights(be_ref, nxt_ref, (wd_hbm,), stage, w_bf, sems, layer=layer, tn=tn)
        y_ref[...] = _dot(h_ref[...], w_bf[0])

    @pl.when(i >= nu_ref[0])
    def _():
        y_ref[...] = jnp.zeros_like(y_ref)


def moe_down(h, w_down, layer, block_e, next_e, n_used, tm, tn=2048):
    p = h.shape[0]
    nb = p // tm
    clamp = lambda i, nu: jnp.minimum(i, nu[0] - 1)
    return pl.pallas_call(
        functools.partial(_moe_down_kernel, layer=layer, tn=tn),
        grid_spec=pltpu.PrefetchScalarGridSpec(
            num_scalar_prefetch=3,
            grid=(D_MODEL // tn, nb),
            in_specs=[pl.BlockSpec((tm, EXPERT_FF), lambda j, i, be, nx, nu: (clamp(i, nu), 0)),
                      pl.BlockSpec(memory_space=pl.ANY)],
            out_specs=pl.BlockSpec((tm, tn), lambda j, i, be, nx, nu: (i, j)),
            scratch_shapes=[pltpu.VMEM((1, EXPERT_FF, tn), F32), pltpu.VMEM((1, EXPERT_FF, tn), BF16),
                            pltpu.SemaphoreType.DMA((1,))]),
        out_shape=jax.ShapeDtypeStruct((p, D_MODEL), F32),
        compiler_params=_cparams(("arbitrary", "arbitrary")),
        name="moe_down",
    )(block_e, next_e, n_used, h, w_down)


def _combine_kernel(pos_ref, pos_next_ref, x_ref, gate_ref, ys_hbm, *rest, tc, split_block):
    out_refs, (buf, sems) = rest[:-2], rest[-2:]
    slot = _prefetched_rows(pos_ref, pos_next_ref, ys_hbm, buf, sems, TOP_K * tc)
    gate = gate_ref[...]
    out = (x_ref[...] + gate[:, 0:1] * buf[slot, pl.ds(0, tc)] + gate[:, 1:2] * buf[slot, pl.ds(tc, tc)])
    if split_block is None:
        out_refs[0][...] = out
    else:
        i = pl.program_id(0)

        @pl.when(i < split_block)
        def _():
            out_refs[0][...] = out

        @pl.when(i >= split_block)
        def _():
            out_refs[1][...] = out


def moe_combine(x, ys, pos, gate, split_rows=None, tc=128):
    n = x.shape[0]
    tc = _tile(n if split_rows is None else math.gcd(split_rows, n - split_rows), tc)
    nb = n // tc
    pos_blk = jnp.swapaxes(pos.reshape(nb, tc, TOP_K), 1, 2).reshape(nb, 1, TOP_K * tc)
    blk = pl.BlockSpec((tc, D_MODEL), lambda i: (i, 0))
    if split_rows is None:
        split_block = None
        out_specs = blk
        out_shape = jax.ShapeDtypeStruct((n, D_MODEL), F32)
    else:
        split_block = split_rows // tc
        out_specs = [pl.BlockSpec((tc, D_MODEL), lambda i: (jnp.minimum(i, split_block - 1), 0)),
                     pl.BlockSpec((tc, D_MODEL), lambda i: (jnp.maximum(i - split_block, 0), 0))]
        out_shape = [jax.ShapeDtypeStruct((split_rows, D_MODEL), F32),
                     jax.ShapeDtypeStruct((n - split_rows, D_MODEL), F32)]
    return pl.pallas_call(
        functools.partial(_combine_kernel, tc=tc, split_block=split_block),
        grid=(nb,),
        in_specs=_index_specs(TOP_K * tc, nb) + [blk, pl.BlockSpec((tc, TOP_K), lambda i: (i, 0)),
                                                 pl.BlockSpec(memory_space=pl.ANY)],
        out_specs=out_specs,
        out_shape=out_shape,
        scratch_shapes=[pltpu.VMEM((2, TOP_K * tc, D_MODEL), F32), pltpu.SemaphoreType.DMA((2,))],
        compiler_params=_cparams(("arbitrary",)),
        name="moe_combine",
    )(pos_blk, pos_blk, x, gate, ys)


def hier_moe_residual(x, norm_g, layer, w_group, b_group, w_router, b_router, w_gate, w_up, w_down,
                      split_rows=None, tm=MOE_BLOCK):
    eid, gate = moe_route(x, norm_g, w_group, b_group, w_router, b_router)
    tok_buf, block_e, next_e, n_used, pos = _dispatch_plan(eid, tm)
    xs = moe_gather_norm(x, norm_g, tok_buf, tm)
    h = moe_up(xs, w_gate, w_up, layer, block_e, next_e, n_used, tm)
    ys = moe_down(h, w_down, layer, block_e, next_e, n_used, tm)
    return moe_combine(x, ys, pos, gate, split_rows)


def encoder_layer(x, layer, segs, lb_f, lb_b, norm_mix, w_in, gmlp_v_norm, gmlp_w_s, gmlp_b_s,
                  hgrn_out_norm, conv_w, mla_q_norm, mla_kv_norm, mla_w_uq, mla_w_ukv, qk_norm_q,
                  qk_norm_k, w_branch, w_out, norm_ffn, w_group, b_group, w_router, b_router,
                  w_gate, w_up, w_down, split_rows=None):
    w_in_l = w_in[layer]
    w_abc = w_in_l[:, :ABC_WIDTH].astype(BF16)
    w_qkv = jnp.pad(w_in_l[:, ABC_WIDTH:ABC_WIDTH + QKV_WIDTH], ((0, 0), (0, QKV_PAD - QKV_WIDTH))).astype(BF16)
    w_gl = w_in_l[:, ABC_WIDTH + QKV_WIDTH:].astype(BF16)

    xn = rmsnorm_rows(x, norm_mix[layer])
    pabc = matmul(xn, w_abc, out_dtype=F32, tm=1024, tn=1024, name="proj_abc")
    pqkv = matmul(xn, w_qkv, out_dtype=F32, tm=1024, tn=384, name="proj_qkv")
    gates = matmul(xn, w_gl, out_dtype=BF16, tm=1024, tn=1024, act="sigmoid", name="proj_gates")

    branches = (
        gmlp_mixer(pabc, gmlp_v_norm[layer], gmlp_w_s[layer], gmlp_b_s[layer]),
        hgrn_mixer(pabc, lb_f, lb_b, hgrn_out_norm[layer], segs),
        conv_mixer(pabc, conv_w[layer], segs),
        mla_mixer(pqkv, mla_q_norm[layer], mla_kv_norm[layer], mla_w_uq[layer], mla_w_ukv[layer],
                  qk_norm_q[layer], qk_norm_k[layer], segs),
    )
    merged = merge_branches(branches, w_branch[layer].astype(BF16), gates)
    x = matmul(merged, w_out[layer].astype(BF16), out_dtype=F32, tm=1024, tn=512, res=x, name="out_proj")
    return hier_moe_residual(x, norm_ffn[layer], layer, w_group[layer], b_group[layer], w_router[layer],
                             b_router[layer], w_gate, w_up, w_down, split_rows)


def kernel(x_prompt, x_sample, norm_mix, w_in, gmlp_v_norm, gmlp_w_s, gmlp_b_s, hgrn_lb_logits, hgrn_out_norm, conv_w, mla_q_norm, mla_kv_norm, mla_w_uq, mla_w_ukv, qk_norm_q, qk_norm_k, w_branch, w_out, norm_ffn, w_group, b_group, w_router, b_router, w_gate, w_up, w_down):
    depth = norm_mix.shape[0]
    bp, sp, d = x_prompt.shape
    bs, ss, _ = x_sample.shape
    n_p, n_s = bp * sp, bs * ss
    segs = ((0, n_p, sp), (n_p, n_s, ss))
    x = (x_prompt.reshape(n_p, d), x_sample.reshape(n_s, d))

    lb_p = jax.nn.softmax(hgrn_lb_logits.astype(F32), axis=1)
    lb = jnp.maximum(jnp.cumsum(lb_p, axis=1) - lb_p[:, :1], 0.0)

    for layer in range(depth):
        x = encoder_layer(x, layer, segs, lb[0, layer], lb[1, layer], norm_mix, w_in, gmlp_v_norm, gmlp_w_s,
                          gmlp_b_s, hgrn_out_norm, conv_w, mla_q_norm, mla_kv_norm, mla_w_uq, mla_w_ukv,
                          qk_norm_q, qk_norm_k, w_branch, w_out, norm_ffn, w_group, b_group, w_router,
                          b_router, w_gate, w_up, w_down, split_rows=n_p if layer == depth - 1 else None)
    y_p, y_s = x
    return (y_p.reshape(bp, sp, d), y_s.reshape(bs, ss, d))
```

```python
import functools
import math

import jax
import jax.numpy as jnp
import numpy as np
from jax import lax
from jax.experimental import pallas as pl
from jax.experimental.pallas import tpu as pltpu

F32 = jnp.float32
BF16 = jnp.bfloat16

D_MODEL = 4096
BRANCH_WIDTH = 1024
HEAD_DIM = 128
N_HEADS = 8
GMLP_CHUNK = 128
MLA_Q_RANK = 768
MLA_KV_RANK = 256
MLA_NOPE = 128
MLA_ROPE = 64
MLA_QK = MLA_NOPE + MLA_ROPE
MLA_QK_PAD = 256
ROPE_BASE = 10000.0
N_GROUPS = 8
EXPERTS_PER_GROUP = 8
N_EXPERTS = 64
TOP_K = 2
EXPERT_FF = 1024
EPS = 1e-6
ABC_WIDTH = 10 * BRANCH_WIDTH
QKV_WIDTH = MLA_Q_RANK + MLA_KV_RANK + MLA_ROPE
QKV_PAD = 1152
LANES = 128
VMEM_LIMIT = 56 * 1024 * 1024
HGRN_CHUNK = 64
MOE_BLOCK = 256
ROWS_PER_ISSUE_STEP = 8
ROW_RING = 3
NEG_INF = float("-inf")


def _cparams(sem):
    return pltpu.CompilerParams(dimension_semantics=sem, vmem_limit_bytes=VMEM_LIMIT)


def _tile(n, t):
    t = min(t, n)
    while n % t:
        t //= 2
    return t


def _dot(a, b):
    return jnp.dot(a, b, preferred_element_type=F32)


def _dot_nt(a, b):
    return lax.dot_general(a, b, (((1,), (1,)), ((), ())), preferred_element_type=F32)


def _rms_rows(x, g):
    return x * lax.rsqrt(jnp.mean(x * x, axis=-1, keepdims=True) + EPS) * g


def _as_parts(x):
    return tuple(x) if isinstance(x, (tuple, list)) else (x,)


def _part_tile(parts, tm):
    return _tile(functools.reduce(math.gcd, [p.shape[0] for p in parts]), tm)


def _part_ranges(parts, tm):
    ranges, off = [], 0
    for p in parts:
        ranges.append((off, p.shape[0] // tm))
        off += p.shape[0] // tm
    return ranges


def _when_in_part(i, off, nblk):
    return pl.when((i >= off) & (i < off + nblk))


def _rmsnorm_kernel(*refs, ranges):
    x_refs, g_ref, o_ref = refs[:-2], refs[-2], refs[-1]
    i = pl.program_id(0)
    for x_ref, (off, nblk) in zip(x_refs, ranges):
        @_when_in_part(i, off, nblk)
        def _(x_ref=x_ref):
            o_ref[...] = _rms_rows(x_ref[...], g_ref[...]).astype(o_ref.dtype)


def rmsnorm_rows(x, g, tm=512):
    parts = _as_parts(x)
    d = parts[0].shape[1]
    tm = _part_tile(parts, tm)
    ranges = _part_ranges(parts, tm)
    n_blocks = sum(nblk for _, nblk in ranges)
    x_specs = [pl.BlockSpec((tm, d), functools.partial(lambda i, off, nblk: (jnp.clip(i - off, 0, nblk - 1), 0),
                                                       off=off, nblk=nblk)) for off, nblk in ranges]
    return pl.pallas_call(
        functools.partial(_rmsnorm_kernel, ranges=ranges),
        grid=(n_blocks,),
        in_specs=x_specs + [pl.BlockSpec((1, d), lambda i: (0, 0))],
        out_specs=pl.BlockSpec((tm, d), lambda i: (i, 0)),
        out_shape=jax.ShapeDtypeStruct((n_blocks * tm, d), BF16),
        compiler_params=_cparams(("parallel",)),
        name="rmsnorm",
    )(*parts, g.reshape(1, d))


def _mm_kernel(x_ref, w_ref, *rest, act, res_ranges):
    res_refs, o_ref = rest[:-1], rest[-1]
    acc = _dot(x_ref[...], w_ref[...])
    if act == "sigmoid":
        acc = jax.nn.sigmoid(acc)
    if not res_refs:
        o_ref[...] = acc.astype(o_ref.dtype)
    i = pl.program_id(0)
    for r_ref, (off, nblk) in zip(res_refs, res_ranges):
        @_when_in_part(i, off, nblk)
        def _(r_ref=r_ref):
            o_ref[...] = (acc + r_ref[...]).astype(o_ref.dtype)


def matmul(x, w, *, out_dtype, tm, tn, act=None, res=None, name="matmul"):
    m, k = x.shape
    n = w.shape[1]
    res_parts = () if res is None else _as_parts(res)
    tm = _part_tile(res_parts + (x,), tm)
    nj = n // tn
    res_ranges = _part_ranges(res_parts, tm)

    def res_index(i, j, off, nblk):
        return (jnp.clip(i - off, 0, nblk - 1), jnp.where(i < off, 0, jnp.where(i >= off + nblk, nj - 1, j)))

    in_specs = [pl.BlockSpec((tm, k), lambda i, j: (i, 0)),
                pl.BlockSpec((k, tn), lambda i, j: (0, j))]
    in_specs += [pl.BlockSpec((tm, tn), functools.partial(res_index, off=off, nblk=nblk))
                 for off, nblk in res_ranges]
    return pl.pallas_call(
        functools.partial(_mm_kernel, act=act, res_ranges=res_ranges),
        grid=(m // tm, nj),
        in_specs=in_specs,
        out_specs=pl.BlockSpec((tm, tn), lambda i, j: (i, j)),
        out_shape=jax.ShapeDtypeStruct((m, n), out_dtype),
        compiler_params=_cparams(("parallel", "parallel")),
        name=name,
    )(x, w, *res_parts)


def _merge_kernel(b0, b1, b2, b3, wb_ref, g0, g1, g2, g3, o_ref):
    acc = None
    for kb, (b_ref, g_ref) in enumerate(((b0, g0), (b1, g1), (b2, g2), (b3, g3))):
        t = _dot(b_ref[...], wb_ref[kb]) * g_ref[...].astype(F32)
        acc = t if acc is None else acc + t
    o_ref[...] = acc.astype(o_ref.dtype)


def merge_branches(branches, w_branch, gates, tm=1024, tn=512):
    n = branches[0].shape[0]
    tm = _tile(n, tm)
    nj = D_MODEL // tn
    b_spec = pl.BlockSpec((tm, BRANCH_WIDTH), lambda i, j: (i, 0))
    g_specs = [pl.BlockSpec((tm, tn), functools.partial(lambda i, j, kb: (i, kb * nj + j), kb=kb))
               for kb in range(4)]
    return pl.pallas_call(
        _merge_kernel,
        grid=(n // tm, nj),
        in_specs=[b_spec] * 4 + [pl.BlockSpec((4, BRANCH_WIDTH, tn), lambda i, j: (0, 0, j))] + g_specs,
        out_specs=pl.BlockSpec((tm, tn), lambda i, j: (i, j)),
        out_shape=jax.ShapeDtypeStruct((n, D_MODEL), BF16),
        compiler_params=_cparams(("parallel", "parallel")),
        name="merge_branches",
    )(*branches, w_branch, gates, gates, gates, gates)


def _gmlp_kernel(u_ref, v_ref, g_ref, ws_ref, bs_ref, o_ref, *, tm):
    for c in range(tm // GMLP_CHUNK):
        rows = pl.ds(c * GMLP_CHUNK, GMLP_CHUNK)
        u = jax.nn.gelu(u_ref[rows, :])
        v = jax.nn.gelu(v_ref[rows, :])
        ms = jnp.mean(v * v, axis=-1, keepdims=True)
        vb = (v * lax.rsqrt(ms + EPS) * g_ref[...]).astype(BF16)
        for h in range(N_HEADS):
            cols = slice(h * HEAD_DIM, (h + 1) * HEAD_DIM)
            mixed = _dot(ws_ref[h], vb[:, cols]) + bs_ref[:, cols]
            o_ref[rows, cols] = (u[:, cols] * mixed).astype(o_ref.dtype)


def gmlp_mixer(pabc, v_gain, w_s, b_s, tm=512):
    n = pabc.shape[0]
    tm = _tile(n, tm)
    bias = jnp.repeat(jnp.swapaxes(b_s, 0, 1), HEAD_DIM, axis=1)
    return pl.pallas_call(
        functools.partial(_gmlp_kernel, tm=tm),
        grid=(n // tm,),
        in_specs=[pl.BlockSpec((tm, BRANCH_WIDTH), lambda i: (i, 0)),
                  pl.BlockSpec((tm, BRANCH_WIDTH), lambda i: (i, 1)),
                  pl.BlockSpec((1, BRANCH_WIDTH), lambda i: (0, 0)),
                  pl.BlockSpec((N_HEADS, GMLP_CHUNK, GMLP_CHUNK), lambda i: (0, 0, 0)),
                  pl.BlockSpec((GMLP_CHUNK, BRANCH_WIDTH), lambda i: (0, 0))],
        out_specs=pl.BlockSpec((tm, BRANCH_WIDTH), lambda i: (i, 0)),
        out_shape=jax.ShapeDtypeStruct((n, BRANCH_WIDTH), BF16),
        compiler_params=_cparams(("parallel",)),
        name="gmlp_mixer",
    )(pabc, pabc, v_gain.reshape(1, -1), w_s.astype(BF16), bias)


def _segment_of(row0, segs):
    seq_len = jnp.int32(segs[-1][2])
    first = jnp.int32(segs[-1][0])
    for (f, nr, s) in reversed(segs[:-1]):
        inside = row0 < f + nr
        seq_len = jnp.where(inside, s, seq_len)
        first = jnp.where(inside, f, first)
    return seq_len, first


def _conv_kernel(bg_ref, cg_ref, h_ref, cgp_ref, hp_ref, cgn_ref, hn_ref, w_ref, o_ref, *, tm, segs):
    row0 = pl.program_id(0) * tm
    seq_len, first = _segment_of(row0, segs)
    at_start = lax.rem(row0 - first, seq_len) == 0
    at_end = lax.rem(row0 - first + tm, seq_len) == 0
    z = cg_ref[...] * h_ref[...]
    left = jnp.where(at_start, 0.0, cgp_ref[7:8, :] * hp_ref[7:8, :])
    right = jnp.where(at_end, 0.0, cgn_ref[0:1, :] * hn_ref[0:1, :])
    ridx = lax.broadcasted_iota(jnp.int32, z.shape, 0)
    z_prev = jnp.where(ridx == 0, left, pltpu.roll(z, 1, axis=0))
    z_next = jnp.where(ridx == tm - 1, right, pltpu.roll(z, tm - 1, axis=0))
    y = w_ref[0:1, :] * z_prev + w_ref[1:2, :] * z + w_ref[2:3, :] * z_next
    o_ref[...] = (bg_ref[...] * y).astype(o_ref.dtype)


def conv_mixer(pabc, conv_w, segs, tm=256):
    n = pabc.shape[0]
    tm = _tile(n, tm)
    nb8 = n // 8
    blk = lambda c: pl.BlockSpec((tm, BRANCH_WIDTH), lambda i: (i, c))
    prev8 = lambda c: pl.BlockSpec((8, BRANCH_WIDTH), lambda i: (jnp.maximum(i * (tm // 8) - 1, 0), c))
    next8 = lambda c: pl.BlockSpec((8, BRANCH_WIDTH), lambda i: (jnp.minimum((i + 1) * (tm // 8), nb8 - 1), c))
    return pl.pallas_call(
        functools.partial(_conv_kernel, tm=tm, segs=segs),
        grid=(n // tm,),
        in_specs=[blk(7), blk(8), blk(9), prev8(8), prev8(9), next8(8), next8(9),
                  pl.BlockSpec((3, BRANCH_WIDTH), lambda i: (0, 0))],
        out_specs=pl.BlockSpec((tm, BRANCH_WIDTH), lambda i: (i, 0)),
        out_shape=jax.ShapeDtypeStruct((n, BRANCH_WIDTH), BF16),
        compiler_params=_cparams(("parallel",)),
        name="conv_mixer",
    )(pabc, pabc, pabc, pabc, pabc, pabc, pabc, conv_w)


def _hgrn_constants(c, reverse):
    n_lev = int(math.log2(c))
    t = np.arange(c)[:, None]
    u = np.arange(c)[None, :]
    if not reverse:
        w_a = u <= t
        w_kd = u > t
    else:
        w_a = u >= t
        w_kd = u < t
    ws = [w_a, w_kd]
    masks = [np.eye(c, dtype=bool)]
    for lev in range(n_lev):
        m = c >> (lev + 1)
        blk_t = t // (2 * m)
        upper_t = (t % (2 * m)) >= m
        p_last_lower = blk_t * 2 * m + m - 1
        if not reverse:
            w_q = (u > p_last_lower) & (u <= t)
            w_k = (u > t) & (u <= p_last_lower)
            w = np.where(upper_t, w_q, w_k)
            query_t = upper_t
        else:
            p_first_upper = p_last_lower + 1
            w_q = (u >= t) & (u < p_first_upper)
            w_k = (u >= p_first_upper) & (u < t)
            w = np.where(upper_t, w_k, w_q)
            query_t = ~upper_t
        ws.append(w)
        same_blk = blk_t == blk_t.T
        masks.append(same_blk & query_t & (~query_t).T)
    return np.stack(ws).astype(np.float32), np.stack(masks).astype(np.float32)


def _hgrn_pair_constants(c):
    ws_f, masks_f = _hgrn_constants(c, False)
    ws_b, masks_b = _hgrn_constants(c, True)

    def blockdiag(a, b):
        out = np.zeros((a.shape[0], 2 * c, 2 * c), np.float32)
        out[:, :c, :c] = a
        out[:, c:, c:] = b
        return out

    w_all = blockdiag(ws_f, ws_b).reshape(-1, 2 * c)
    return jnp.asarray(w_all, BF16), jnp.asarray(blockdiag(masks_f, masks_b), F32)


def _log_forget_and_key(z, log_lb, log1m_lb):
    log_sig = jnp.minimum(z, 0.0) - jnp.log1p(jnp.exp(-jnp.abs(z)))
    x2 = log1m_lb + log_sig
    hi = jnp.maximum(log_lb, x2)
    lf = hi + jnp.log1p(jnp.exp(-jnp.abs(log_lb - x2)))
    return lf, jnp.exp(x2 - z)


def _split3(x):
    p1 = x.astype(BF16)
    r1 = x - p1.astype(F32)
    p2 = r1.astype(BF16)
    p3 = (r1 - p2.astype(F32)).astype(BF16)
    return p1, p2, p3


def _hgrn_kernel(qf_ref, vf_ref, zf_ref, qb_ref, vb_ref, zb_ref, lb_ref, wall_ref, mask_ref,
                 of_ref, ob_ref, st_ref, *, c, n_lev, segs):
    step = pl.program_id(0)
    row_f = step * c
    row_b = (pl.num_programs(0) - 1 - step) * c
    len_f, first_f = _segment_of(row_f, segs)
    len_b, first_b = _segment_of(row_b, segs)

    @pl.when(lax.rem(row_f - first_f, len_f) == 0)
    def _():
        st_ref[0] = jnp.zeros(st_ref.shape[1:], F32)

    @pl.when(lax.rem(row_b - first_b + c, len_b) == 0)
    def _():
        st_ref[1] = jnp.zeros(st_ref.shape[1:], F32)

    w_all = wall_ref[...]
    q = jnp.concatenate([qf_ref[...], qb_ref[...]], axis=0)
    v = jnp.concatenate([vf_ref[...], vb_ref[...]], axis=0)
    z = jnp.concatenate([zf_ref[...], zb_ref[...]], axis=0)
    row = lax.broadcasted_iota(jnp.int32, q.shape, 0)
    is_fwd = row < c
    lf, k = _log_forget_and_key(z, jnp.where(is_fwd, lb_ref[0:1, :], lb_ref[2:3, :]),
                                jnp.where(is_fwd, lb_ref[1:2, :], lb_ref[3:4, :]))
    p1, p2, p3 = _split3(lf)
    e_all = _dot(w_all, p1) + _dot(w_all, p2) + _dot(w_all, p3)
    c2 = 2 * c
    a = e_all[0:c2]
    qd = (q * jnp.exp(a)).astype(BF16)
    kd = (k * jnp.exp(e_all[c2:2 * c2])).astype(BF16)
    q16 = q.astype(BF16)
    k16 = k.astype(BF16)
    v16 = v.astype(BF16)
    xms = []
    for lev in range(n_lev):
        upper = ((row >> (n_lev - 1 - lev)) & 1) == 1
        is_query = upper == is_fwd
        dm = jnp.exp(e_all[(2 + lev) * c2:(3 + lev) * c2])
        xms.append((jnp.where(is_query, q, k) * dm).astype(BF16))
    decay_f = jnp.exp(a[c - 1:c])
    decay_b = jnp.exp(a[c:c + 1])
    for h in range(N_HEADS):
        cols = slice(h * HEAD_DIM, (h + 1) * HEAD_DIM)
        scores = mask_ref[0] * _dot_nt(q16[:, cols], k16[:, cols])
        for lev in range(n_lev):
            xm = xms[lev][:, cols]
            scores = scores + mask_ref[lev + 1] * _dot_nt(xm, xm)
        o_intra = _dot(scores.astype(BF16), v16[:, cols])
        st_f = st_ref[0, h]
        st_b = st_ref[1, h]
        of_ref[:, cols] = o_intra[:c] + _dot_nt(qd[:c, cols], st_f.astype(BF16))
        ob_ref[:, cols] = o_intra[c:] + _dot_nt(qd[c:, cols], st_b.astype(BF16))
        st_ref[0, h] = st_f * decay_f[:, cols] + _dot(v[:c, cols].T.astype(BF16), kd[:c, cols])
        st_ref[1, h] = st_b * decay_b[:, cols] + _dot(v[c:, cols].T.astype(BF16), kd[c:, cols])


def _hgrn_out_kernel(of_ref, ob_ref, g_ref, gain_ref, o_ref):
    for h in range(N_HEADS):
        cols = slice(h * HEAD_DIM, (h + 1) * HEAD_DIM)
        o = of_ref[:, cols] + ob_ref[:, cols]
        ms = jnp.mean(o * o, axis=-1, keepdims=True)
        o_ref[:, cols] = (o * lax.rsqrt(ms + EPS) * gain_ref[...] * jax.nn.silu(g_ref[:, cols])).astype(o_ref.dtype)


def hgrn_mixer(pabc, lb_f, lb_b, out_gain, segs, c=HGRN_CHUNK, tm=512):
    n = pabc.shape[0]
    n_lev = int(math.log2(c))
    n_chunks = n // c
    w_all, masks = _hgrn_pair_constants(c)
    log_lb = jnp.stack([jnp.log(lb_f), jnp.log1p(-lb_f), jnp.log(lb_b), jnp.log1p(-lb_b)])
    fwd = lambda col: pl.BlockSpec((c, BRANCH_WIDTH), lambda i: (i, col))
    bwd = lambda col: pl.BlockSpec((c, BRANCH_WIDTH), lambda i: (n_chunks - 1 - i, col))
    o_f, o_b = pl.pallas_call(
        functools.partial(_hgrn_kernel, c=c, n_lev=n_lev, segs=segs),
        grid=(n_chunks,),
        in_specs=[fwd(2), fwd(3), fwd(4), bwd(2), bwd(3), bwd(5),
                  pl.BlockSpec((4, BRANCH_WIDTH), lambda i: (0, 0)),
                  pl.BlockSpec(w_all.shape, lambda i: (0, 0)),
                  pl.BlockSpec(masks.shape, lambda i: (0, 0, 0))],
        out_specs=[fwd(0), bwd(0)],
        out_shape=[jax.ShapeDtypeStruct((n, BRANCH_WIDTH), F32)] * 2,
        scratch_shapes=[pltpu.VMEM((2, N_HEADS, HEAD_DIM, HEAD_DIM), F32)],
        compiler_params=_cparams(("arbitrary",)),
        name="hgrn_scan",
    )(pabc, pabc, pabc, pabc, pabc, pabc, log_lb, w_all, masks)
    tm = _tile(n, tm)
    blk = lambda col: pl.BlockSpec((tm, BRANCH_WIDTH), lambda i: (i, col))
    return pl.pallas_call(
        _hgrn_out_kernel,
        grid=(n // tm,),
        in_specs=[blk(0), blk(0), blk(6), pl.BlockSpec((1, HEAD_DIM), lambda i: (0, 0))],
        out_specs=blk(0),
        out_shape=jax.ShapeDtypeStruct((n, BRANCH_WIDTH), BF16),
        compiler_params=_cparams(("parallel",)),
        name="hgrn_out",
    )(o_f, o_b, pabc, out_gain.reshape(1, HEAD_DIM))


def _rope_tile(r, cos_t, sin_a, sin_b):
    return r * cos_t + pltpu.roll(r, 96, axis=1) * sin_a + pltpu.roll(r, 32, axis=1) * sin_b


def _mla_prep_kernel(cq_ref, ckv_ref, wq_ref, wkv_ref, qg_ref, kvg_ref, gq_ref, gk_ref,
                     cos_ref, sina_ref, sinb_ref, q_out, k_out, v_out):
    cq = cq_ref[...]
    hq = (cq * lax.rsqrt(jnp.mean(cq * cq, axis=-1, keepdims=True) + EPS) * qg_ref[...]).astype(BF16)
    qf = _dot(hq, wq_ref[...])
    ckvkr = ckv_ref[...]
    ckv = ckvkr[:, :MLA_KV_RANK]
    kr = ckvkr[:, MLA_KV_RANK:]
    hk = (ckv * lax.rsqrt(jnp.mean(ckv * ckv, axis=-1, keepdims=True) + EPS) * kvg_ref[...]).astype(BF16)
    kvf = _dot(hk, wkv_ref[...])
    cos_t, sin_a, sin_b = cos_ref[...], sina_ref[...], sinb_ref[...]
    gq_n, gq_r = gq_ref[:, :MLA_NOPE], gq_ref[:, MLA_NOPE:]
    gk_n, gk_r = gk_ref[:, :MLA_NOPE], gk_ref[:, MLA_NOPE:]
    kr_ss = jnp.sum(kr * kr, axis=-1, keepdims=True)
    kr_rot = _rope_tile(kr * gk_r, cos_t, sin_a, sin_b)
    scale = MLA_QK ** -0.5 * math.log2(math.e)
    lane = lax.broadcasted_iota(jnp.int32, (cq.shape[0], HEAD_DIM), 1)
    ones_col = jnp.where(lane == 0, 1.0, 0.0).astype(BF16)
    for h in range(N_HEADS):
        base = h * MLA_QK_PAD
        qn = qf[:, base:base + MLA_NOPE]
        qr = qf[:, base + MLA_NOPE:base + MLA_QK_PAD]
        ss = jnp.sum(qn * qn, axis=-1, keepdims=True) + jnp.sum(qr * qr, axis=-1, keepdims=True)
        rstd = lax.rsqrt(ss * (1.0 / MLA_QK) + EPS) * scale
        q_out[h, :, :MLA_NOPE] = (qn * rstd * gq_n).astype(BF16)
        q_out[h, :, MLA_NOPE:] = _rope_tile(qr * rstd * gq_r, cos_t, sin_a, sin_b).astype(BF16)
        kn = kvf[:, base:base + MLA_NOPE]
        ssk = jnp.sum(kn * kn, axis=-1, keepdims=True) + kr_ss
        rstdk = lax.rsqrt(ssk * (1.0 / MLA_QK) + EPS)
        k_out[h, :, :MLA_NOPE] = (kn * rstdk * gk_n).astype(BF16)
        k_out[h, :, MLA_NOPE:] = (kr_rot * rstdk).astype(BF16)
        v_out[h, :, :HEAD_DIM] = kvf[:, base + MLA_NOPE:base + MLA_QK_PAD].astype(BF16)
        v_out[h, :, HEAD_DIM:] = ones_col


def _rope_tables(segs):
    half = MLA_ROPE // 2
    inv_freq = ROPE_BASE ** (-jnp.arange(half, dtype=F32) * 2.0 / MLA_ROPE)
    pos = jnp.concatenate([jnp.tile(jnp.arange(s, dtype=F32), nr // s) for (_, nr, s) in segs])
    ang = pos[:, None] * inv_freq[None, :]
    cos, sin = jnp.cos(ang), jnp.sin(ang)
    zero = jnp.zeros_like(cos)
    zero2 = jnp.concatenate([zero, zero], axis=1)
    cos_t = jnp.concatenate([cos, cos, zero2], axis=1)
    sin_a = jnp.concatenate([-sin, zero, zero2], axis=1)
    sin_b = jnp.concatenate([zero, sin, zero2], axis=1)
    return cos_t, sin_a, sin_b


def mla_prep(pqkv, q_gain, kv_gain, w_uq, w_ukv, gq, gk, segs, tm=256):
    n = pqkv.shape[0]
    tm = _tile(n, tm)
    wq = jnp.pad(w_uq.reshape(MLA_Q_RANK, N_HEADS, MLA_QK),
                 ((0, 0), (0, 0), (0, MLA_QK_PAD - MLA_QK))).reshape(MLA_Q_RANK, -1).astype(BF16)
    wkv = w_ukv.astype(BF16)
    pad = lambda g: jnp.pad(g, (0, MLA_QK_PAD - MLA_QK)).reshape(1, MLA_QK_PAD)
    cos_t, sin_a, sin_b = _rope_tables(segs)
    const = lambda shape: pl.BlockSpec(shape, lambda i: (0, 0))
    tab = pl.BlockSpec((tm, LANES), lambda i: (i, 0))
    hd = lambda w: pl.BlockSpec((N_HEADS, tm, w), lambda i: (0, i, 0))
    return pl.pallas_call(
        _mla_prep_kernel,
        grid=(n // tm,),
        in_specs=[pl.BlockSpec((tm, MLA_Q_RANK), lambda i: (i, 0)),
                  pl.BlockSpec((tm, QKV_PAD - MLA_Q_RANK), lambda i: (i, 2)),
                  const(wq.shape), const(wkv.shape),
                  const((1, MLA_Q_RANK)), const((1, MLA_KV_RANK)),
                  const((1, MLA_QK_PAD)), const((1, MLA_QK_PAD)), tab, tab, tab],
        out_specs=[hd(MLA_QK_PAD), hd(MLA_QK_PAD), hd(2 * HEAD_DIM)],
        out_shape=[jax.ShapeDtypeStruct((N_HEADS, n, MLA_QK_PAD), BF16),
                   jax.ShapeDtypeStruct((N_HEADS, n, MLA_QK_PAD), BF16),
                   jax.ShapeDtypeStruct((N_HEADS, n, 2 * HEAD_DIM), BF16)],
        compiler_params=_cparams(("parallel",)),
        name="mla_prep",
    )(pqkv, pqkv, wq, wkv, q_gain.reshape(1, -1), kv_gain.reshape(1, -1), pad(gq), pad(gk),
      cos_t, sin_a, sin_b)


def _attn_kernel(q_ref, k_ref, v_ref, o_ref, *, kv_chunk):
    q = q_ref[...]
    acc = None
    m_run = None
    for c0 in range(0, k_ref.shape[0], kv_chunk):
        s = _dot_nt(q, k_ref[pl.ds(c0, kv_chunk), :])
        m = jnp.max(s, axis=-1, keepdims=True)
        ol = _dot(jnp.exp2(s - m).astype(BF16), v_ref[pl.ds(c0, kv_chunk), :])
        if acc is None:
            acc, m_run = ol, m
        else:
            m_new = jnp.maximum(m_run, m)
            acc = acc * jnp.exp2(m_run - m_new) + ol * jnp.exp2(m - m_new)
            m_run = m_new
    o_ref[...] = (acc[:, :HEAD_DIM] / acc[:, HEAD_DIM:HEAD_DIM + 1]).astype(o_ref.dtype)


def attention_segment(q, k, v, first_row, n_rows, seq_len, tq=512, kv_chunk=1024):
    tq = min(tq, seq_len)
    kv_chunk = min(kv_chunk, seq_len)
    n_seq = n_rows // seq_len
    qb_per_seq = seq_len // tq
    qoff = first_row // tq
    soff = first_row // seq_len
    return pl.pallas_call(
        functools.partial(_attn_kernel, kv_chunk=kv_chunk),
        grid=(n_seq, N_HEADS, qb_per_seq),
        in_specs=[pl.BlockSpec((None, tq, MLA_QK_PAD), lambda b, h, i: (h, qoff + b * qb_per_seq + i, 0)),
                  pl.BlockSpec((None, seq_len, MLA_QK_PAD), lambda b, h, i: (h, soff + b, 0)),
                  pl.BlockSpec((None, seq_len, 2 * HEAD_DIM), lambda b, h, i: (h, soff + b, 0))],
        out_specs=pl.BlockSpec((tq, HEAD_DIM), lambda b, h, i: (b * qb_per_seq + i, h)),
        out_shape=jax.ShapeDtypeStruct((n_rows, BRANCH_WIDTH), BF16),
        compiler_params=_cparams(("parallel", "parallel", "parallel")),
        name="mla_attention",
    )(q, k, v)


def mla_mixer(pqkv, q_gain, kv_gain, w_uq, w_ukv, gq, gk, segs):
    q, k, v = mla_prep(pqkv, q_gain, kv_gain, w_uq, w_ukv, gq, gk, segs)
    outs = [attention_segment(q, k, v, f, nr, s) for (f, nr, s) in segs]
    return outs[0] if len(outs) == 1 else jnp.concatenate(outs, axis=0)


def _route_kernel(x_ref, g_ref, w_ref, b_ref, o_ref):
    xn = _rms_rows(x_ref[...], g_ref[...]).astype(BF16)
    lg = _dot(xn, w_ref[...]) + b_ref[...]
    lane = lax.broadcasted_iota(jnp.int32, lg.shape, 1)
    big = jnp.int32(1 << 20)
    is_g = (lane >= N_EXPERTS) & (lane < N_EXPERTS + N_GROUPS)
    gl = jnp.where(is_g, lg, NEG_INF)
    gmax = jnp.max(gl, axis=-1, keepdims=True)
    gidx = jnp.min(jnp.where(is_g & (gl == gmax), lane - N_EXPERTS, big), axis=-1, keepdims=True)
    gsum = jnp.sum(jnp.where(is_g, jnp.exp(gl - gmax), 0.0), axis=-1, keepdims=True)
    g_top = 1.0 / gsum
    in_grp = (lane < N_EXPERTS) & ((lane >> 3) == gidx)
    el = jnp.where(in_grp, lg, NEG_INF)
    v1 = jnp.max(el, axis=-1, keepdims=True)
    i1 = jnp.min(jnp.where(in_grp & (el == v1), lane, big), axis=-1, keepdims=True)
    rest = in_grp & (lane != i1)
    el2 = jnp.where(rest, lg, NEG_INF)
    v2 = jnp.max(el2, axis=-1, keepdims=True)
    i2 = jnp.min(jnp.where(rest & (el2 == v2), lane, big), axis=-1, keepdims=True)
    d = jnp.exp(v2 - v1)
    p1 = 1.0 / (1.0 + d)
    p2 = d / (1.0 + d)
    out = jnp.where(lane == 0, i1.astype(F32),
                    jnp.where(lane == 1, i2.astype(F32),
                              jnp.where(lane == 2, g_top * p1,
                                        jnp.where(lane == 3, g_top * p2, 0.0))))
    o_ref[...] = out


def moe_route(x, norm_g, w_group, b_group, w_router, b_router, tm=512):
    n = x.shape[0]
    tm = _tile(n, tm)
    padc = LANES - N_EXPERTS - N_GROUPS
    w = jnp.concatenate([w_router, w_group, jnp.zeros((D_MODEL, padc), F32)], axis=1).astype(BF16)
    b = jnp.concatenate([b_router, b_group, jnp.zeros((padc,), F32)]).reshape(1, LANES)
    r = pl.pallas_call(
        _route_kernel,
        grid=(n // tm,),
        in_specs=[pl.BlockSpec((tm, D_MODEL), lambda i: (i, 0)),
                  pl.BlockSpec((1, D_MODEL), lambda i: (0, 0)),
                  pl.BlockSpec((D_MODEL, LANES), lambda i: (0, 0)),
                  pl.BlockSpec((1, LANES), lambda i: (0, 0))],
        out_specs=pl.BlockSpec((tm, LANES), lambda i: (i, 0)),
        out_shape=jax.ShapeDtypeStruct((n, LANES), F32),
        compiler_params=_cparams(("parallel",)),
        name="moe_route",
    )(x, norm_g.reshape(1, D_MODEL), w, b)
    return r[:, :TOP_K].astype(jnp.int32), r[:, TOP_K:2 * TOP_K]


def _dispatch_plan(eid, tm):
    n = eid.shape[0]
    m = n * TOP_K
    flat_e = eid.reshape(-1)
    onehot = (flat_e[:, None] == jnp.arange(N_EXPERTS, dtype=jnp.int32)[None, :]).astype(jnp.int32)
    csum = jnp.cumsum(onehot, axis=0)
    rank = jnp.take_along_axis(csum, flat_e[:, None], axis=1)[:, 0] - 1
    counts = csum[-1]
    padded = (counts + tm - 1) // tm * tm
    pad_end = jnp.cumsum(padded)
    pad_start = pad_end - padded
    dest = pad_start[flat_e] + rank
    p = (m + tm - 1) // tm * tm + N_EXPERTS * tm
    tok_buf = jnp.zeros((p,), jnp.int32).at[dest].set(jnp.arange(m, dtype=jnp.int32) // TOP_K)
    n_blocks = p // tm
    block_e = jnp.minimum(jnp.searchsorted(pad_end, jnp.arange(n_blocks, dtype=jnp.int32) * tm, side="right"),
                          N_EXPERTS - 1).astype(jnp.int32)
    n_used = (pad_end[-1] // tm).astype(jnp.int32).reshape(1)
    experts = jnp.arange(N_EXPERTS, dtype=jnp.int32)
    first_nonempty_from = lax.cummin(jnp.where(counts > 0, experts, N_EXPERTS)[::-1])[::-1]
    next_nonempty = jnp.concatenate([first_nonempty_from[1:], jnp.full((1,), N_EXPERTS, jnp.int32)])
    next_e = jnp.where(next_nonempty < N_EXPERTS, next_nonempty, -1)[block_e].astype(jnp.int32)
    return tok_buf, block_e, next_e, n_used, dest.reshape(n, TOP_K).astype(jnp.int32)


def _prefetched_rows(idx_refs, src_hbm, buf, sems, n_rows):
    i = pl.program_id(0)
    n_steps = pl.num_programs(0)
    ahead = ROW_RING - 1

    def start_block(idx, s):
        def body(r8, carry):
            for k in range(ROWS_PER_ISSUE_STEP):
                r = ROWS_PER_ISSUE_STEP * r8 + k
                pltpu.make_async_copy(src_hbm.at[pl.ds(idx[0, 0, r], 1)], buf.at[s, pl.ds(r, 1)],
                                      sems.at[s]).start(priority=k % 2)
            return carry
        lax.fori_loop(0, n_rows // ROWS_PER_ISSUE_STEP, body, 0)

    for d in range(ahead):
        @pl.when((i == 0) & (d < n_steps))
        def _(d=d):
            start_block(idx_refs[d], d)

    @pl.when(i + ahead < n_steps)
    def _():
        start_block(idx_refs[ahead], lax.rem(i + ahead, ROW_RING))

    slot = lax.rem(i, ROW_RING)
    pltpu.make_async_copy(src_hbm.at[pl.ds(0, n_rows)], buf.at[slot], sems.at[slot]).wait()
    return slot


def _index_specs(n_rows, nb):
    return [pl.BlockSpec((1, 1, n_rows), functools.partial(lambda i, d: (jnp.minimum(i + d, nb - 1), 0, 0), d=d),
                         memory_space=pltpu.SMEM) for d in range(ROW_RING)]


def _gather_norm_kernel(*refs, rows):
    idx_refs, (x_hbm, g_ref, o_ref, buf, sems) = refs[:ROW_RING], refs[ROW_RING:]
    slot = _prefetched_rows(idx_refs, x_hbm, buf, sems, rows)
    o_ref[...] = _rms_rows(buf[slot], g_ref[...]).astype(o_ref.dtype)


def moe_gather_norm(x, norm_g, tok_buf, tm):
    p = tok_buf.shape[0]
    nb = p // tm
    tok = tok_buf.reshape(nb, 1, tm)
    return pl.pallas_call(
        functools.partial(_gather_norm_kernel, rows=tm),
        grid=(nb,),
        in_specs=_index_specs(tm, nb) + [pl.BlockSpec(memory_space=pl.ANY),
                                         pl.BlockSpec((1, D_MODEL), lambda i: (0, 0))],
        out_specs=pl.BlockSpec((tm, D_MODEL), lambda i: (i, 0)),
        scratch_shapes=[pltpu.VMEM((ROW_RING, tm, D_MODEL), F32), pltpu.SemaphoreType.DMA((ROW_RING,))],
        out_shape=jax.ShapeDtypeStruct((p, D_MODEL), BF16),
        compiler_params=_cparams(("arbitrary",)),
        name="moe_gather_norm",
    )(*([tok] * ROW_RING), x, norm_g.reshape(1, D_MODEL))


def _expert_weights(be_ref, nxt_ref, w_hbms, stage, w_bf, sems, *, layer, tn):
    j = pl.program_id(0)
    i = pl.program_id(1)
    e = be_ref[i]

    def copies(expert, jj):
        col = pl.multiple_of(jj * tn, tn)
        return [pltpu.make_async_copy(w.at[layer, expert, :, pl.ds(col, tn)], stage.at[k], sems.at[k])
                for k, w in enumerate(w_hbms)]

    def start(expert, jj):
        for cp in copies(expert, jj):
            cp.start()

    @pl.when((i == 0) | (be_ref[jnp.maximum(i - 1, 0)] != e))
    def _():
        @pl.when((i == 0) & (j == 0))
        def _():
            start(e, j)

        for cp in copies(e, j):
            cp.wait()
        for k in range(len(w_hbms)):
            w_bf[k] = stage[k].astype(BF16)
        nxt = nxt_ref[i]

        @pl.when(nxt >= 0)
        def _():
            start(nxt, j)

        @pl.when((nxt < 0) & (j + 1 < pl.num_programs(0)))
        def _():
            start(be_ref[0], j + 1)


def _moe_up_kernel(be_ref, nxt_ref, nu_ref, xs_ref, wg_hbm, wu_hbm, h_ref, stage, w_bf, sems, *, layer, tn):
    i = pl.program_id(1)

    @pl.when(i < nu_ref[0])
    def _():
        _expert_weights(be_ref, nxt_ref, (wg_hbm, wu_hbm), stage, w_bf, sems, layer=layer, tn=tn)
        x = xs_ref[...]
        g = _dot(x, w_bf[0])
        u = _dot(x, w_bf[1])
        h_ref[...] = (jax.nn.silu(g) * u).astype(h_ref.dtype)

    @pl.when(i >= nu_ref[0])
    def _():
        h_ref[...] = jnp.zeros_like(h_ref)


def moe_up(xs, w_gate, w_up, layer, block_e, next_e, n_used, tm, tn=512):
    p = xs.shape[0]
    nb = p // tm
    clamp = lambda i, nu: jnp.minimum(i, nu[0] - 1)
    return pl.pallas_call(
        functools.partial(_moe_up_kernel, layer=layer, tn=tn),
        grid_spec=pltpu.PrefetchScalarGridSpec(
            num_scalar_prefetch=3,
            grid=(EXPERT_FF // tn, nb),
            in_specs=[pl.BlockSpec((tm, D_MODEL), lambda j, i, be, nx, nu: (clamp(i, nu), 0)),
                      pl.BlockSpec(memory_space=pl.ANY), pl.BlockSpec(memory_space=pl.ANY)],
            out_specs=pl.BlockSpec((tm, tn), lambda j, i, be, nx, nu: (i, j)),
            scratch_shapes=[pltpu.VMEM((2, D_MODEL, tn), F32), pltpu.VMEM((2, D_MODEL, tn), BF16),
                            pltpu.SemaphoreType.DMA((2,))]),
        out_shape=jax.ShapeDtypeStruct((p, EXPERT_FF), BF16),
        compiler_params=_cparams(("arbitrary", "arbitrary")),
        name="moe_up",
    )(block_e, next_e, n_used, xs, w_gate, w_up)


def _moe_down_kernel(be_ref, nxt_ref, nu_ref, h_ref, wd_hbm, y_ref, stage, w_bf, sems, *, layer, tn):
    i = pl.program_id(1)

    @pl.when(i < nu_ref[0])
    def _():
        _expert_weights(be_ref, nxt_ref, (wd_hbm,), stage, w_bf, sems, layer=layer, tn=tn)
        y_ref[...] = _dot(h_ref[...], w_bf[0])

    @pl.when(i >= nu_ref[0])
    def _():
        y_ref[...] = jnp.zeros_like(y_ref)


def moe_down(h, w_down, layer, block_e, next_e, n_used, tm, tn=2048):
    p = h.shape[0]
    nb = p // tm
    clamp = lambda i, nu: jnp.minimum(i, nu[0] - 1)
    return pl.pallas_call(
        functools.partial(_moe_down_kernel, layer=layer, tn=tn),
        grid_spec=pltpu.PrefetchScalarGridSpec(
            num_scalar_prefetch=3,
            grid=(D_MODEL // tn, nb),
            in_specs=[pl.BlockSpec((tm, EXPERT_FF), lambda j, i, be, nx, nu: (clamp(i, nu), 0)),
                      pl.BlockSpec(memory_space=pl.ANY)],
            out_specs=pl.BlockSpec((tm, tn), lambda j, i, be, nx, nu: (i, j)),
            scratch_shapes=[pltpu.VMEM((1, EXPERT_FF, tn), F32), pltpu.VMEM((1, EXPERT_FF, tn), BF16),
                            pltpu.SemaphoreType.DMA((1,))]),
        out_shape=jax.ShapeDtypeStruct((p, D_MODEL), F32),
        compiler_params=_cparams(("arbitrary", "arbitrary")),
        name="moe_down",
    )(block_e, next_e, n_used, h, w_down)


def _combine_kernel(*refs, tc, split_block):
    idx_refs, (x_ref, gate_ref, ys_hbm), rest = refs[:ROW_RING], refs[ROW_RING:ROW_RING + 3], refs[ROW_RING + 3:]
    out_refs, (buf, sems) = rest[:-2], rest[-2:]
    slot = _prefetched_rows(idx_refs, ys_hbm, buf, sems, TOP_K * tc)
    gate = gate_ref[...]
    out = (x_ref[...] + gate[:, 0:1] * buf[slot, pl.ds(0, tc)] + gate[:, 1:2] * buf[slot, pl.ds(tc, tc)])
    if split_block is None:
        out_refs[0][...] = out
    else:
        i = pl.program_id(0)

        @pl.when(i < split_block)
        def _():
            out_refs[0][...] = out

        @pl.when(i >= split_block)
        def _():
            out_refs[1][...] = out


def moe_combine(x, ys, pos, gate, split_rows=None, tc=128):
    n = x.shape[0]
    tc = _tile(n if split_rows is None else math.gcd(split_rows, n - split_rows), tc)
    nb = n // tc
    pos_blk = jnp.swapaxes(pos.reshape(nb, tc, TOP_K), 1, 2).reshape(nb, 1, TOP_K * tc)
    blk = pl.BlockSpec((tc, D_MODEL), lambda i: (i, 0))
    if split_rows is None:
        split_block = None
        out_specs = blk
        out_shape = jax.ShapeDtypeStruct((n, D_MODEL), F32)
    else:
        split_block = split_rows // tc
        out_specs = [pl.BlockSpec((tc, D_MODEL), lambda i: (jnp.minimum(i, split_block - 1), 0)),
                     pl.BlockSpec((tc, D_MODEL), lambda i: (jnp.maximum(i - split_block, 0), 0))]
        out_shape = [jax.ShapeDtypeStruct((split_rows, D_MODEL), F32),
                     jax.ShapeDtypeStruct((n - split_rows, D_MODEL), F32)]
    return pl.pallas_call(
        functools.partial(_combine_kernel, tc=tc, split_block=split_block),
        grid=(nb,),
        in_specs=_index_specs(TOP_K * tc, nb) + [blk, pl.BlockSpec((tc, TOP_K), lambda i: (i, 0)),
                                                 pl.BlockSpec(memory_space=pl.ANY)],
        out_specs=out_specs,
        out_shape=out_shape,
        scratch_shapes=[pltpu.VMEM((ROW_RING, TOP_K * tc, D_MODEL), F32), pltpu.SemaphoreType.DMA((ROW_RING,))],
        compiler_params=_cparams(("arbitrary",)),
        name="moe_combine",
    )(*([pos_blk] * ROW_RING), x, gate, ys)


def hier_moe_residual(x, norm_g, layer, w_group, b_group, w_router, b_router, w_gate, w_up, w_down,
                      split_rows=None, tm=MOE_BLOCK):
    eid, gate = moe_route(x, norm_g, w_group, b_group, w_router, b_router)
    tok_buf, block_e, next_e, n_used, pos = _dispatch_plan(eid, tm)
    xs = moe_gather_norm(x, norm_g, tok_buf, tm)
    h = moe_up(xs, w_gate, w_up, layer, block_e, next_e, n_used, tm)
    ys = moe_down(h, w_down, layer, block_e, next_e, n_used, tm)
    return moe_combine(x, ys, pos, gate, split_rows)


def encoder_layer(x, layer, segs, lb_f, lb_b, norm_mix, w_in, gmlp_v_norm, gmlp_w_s, gmlp_b_s,
                  hgrn_out_norm, conv_w, mla_q_norm, mla_kv_norm, mla_w_uq, mla_w_ukv, qk_norm_q,
                  qk_norm_k, w_branch, w_out, norm_ffn, w_group, b_group, w_router, b_router,
                  w_gate, w_up, w_down, split_rows=None):
    w_in_l = w_in[layer]
    w_abc = w_in_l[:, :ABC_WIDTH].astype(BF16)
    w_qkv = jnp.pad(w_in_l[:, ABC_WIDTH:ABC_WIDTH + QKV_WIDTH], ((0, 0), (0, QKV_PAD - QKV_WIDTH))).astype(BF16)
    w_gl = w_in_l[:, ABC_WIDTH + QKV_WIDTH:].astype(BF16)

    xn = rmsnorm_rows(x, norm_mix[layer])
    pabc = matmul(xn, w_abc, out_dtype=F32, tm=1024, tn=1024, name="proj_abc")
    pqkv = matmul(xn, w_qkv, out_dtype=F32, tm=1024, tn=384, name="proj_qkv")
    gates = matmul(xn, w_gl, out_dtype=BF16, tm=1024, tn=1024, act="sigmoid", name="proj_gates")

    branches = (
        gmlp_mixer(pabc, gmlp_v_norm[layer], gmlp_w_s[layer], gmlp_b_s[layer]),
        hgrn_mixer(pabc, lb_f, lb_b, hgrn_out_norm[layer], segs),
        conv_mixer(pabc, conv_w[layer], segs),
        mla_mixer(pqkv, mla_q_norm[layer], mla_kv_norm[layer], mla_w_uq[layer], mla_w_ukv[layer],
                  qk_norm_q[layer], qk_norm_k[layer], segs),
    )
    merged = merge_branches(branches, w_branch[layer].astype(BF16), gates)
    x = matmul(merged, w_out[layer].astype(BF16), out_dtype=F32, tm=1024, tn=512, res=x, name="out_proj")
    return hier_moe_residual(x, norm_ffn[layer], layer, w_group[layer], b_group[layer], w_router[layer],
                             b_router[layer], w_gate, w_up, w_down, split_rows)


def kernel(x_prompt, x_sample, norm_mix, w_in, gmlp_v_norm, gmlp_w_s, gmlp_b_s, hgrn_lb_logits, hgrn_out_norm, conv_w, mla_q_norm, mla_kv_norm, mla_w_uq, mla_w_ukv, qk_norm_q, qk_norm_k, w_branch, w_out, norm_ffn, w_group, b_group, w_router, b_router, w_gate, w_up, w_down):
    depth = norm_mix.shape[0]
    bp, sp, d = x_prompt.shape
    bs, ss, _ = x_sample.shape
    n_p, n_s = bp * sp, bs * ss
    segs = ((0, n_p, sp), (n_p, n_s, ss))
    x = (x_prompt.reshape(n_p, d), x_sample.reshape(n_s, d))

    lb_p = jax.nn.softmax(hgrn_lb_logits.astype(F32), axis=1)
    lb = jnp.maximum(jnp.cumsum(lb_p, axis=1) - lb_p[:, :1], 0.0)

    for layer in range(depth):
        x = encoder_layer(x, layer, segs, lb[0, layer], lb[1, layer], norm_mix, w_in, gmlp_v_norm, gmlp_w_s,
                          gmlp_b_s, hgrn_out_norm, conv_w, mla_q_norm, mla_kv_norm, mla_w_uq, mla_w_ukv,
                          qk_norm_q, qk_norm_k, w_branch, w_out, norm_ffn, w_group, b_group, w_router,
                          b_router, w_gate, w_up, w_down, split_rows=n_p if layer == depth - 1 else None)
    y_p, y_s = x
    return (y_p.reshape(bp, sp, d), y_s.reshape(bs, ss, d))
```
